```python
import jax
import jax.numpy as jnp
from jax import lax
import numpy as np

D_MODEL = 4096
BATCH = 2
SEQ = 8192
DEPTH = 4
DEC_BATCH = 8
DEC_SEQ = 64
PAST_LEN = 2048

CHUNK = 64
N_EVEN = (DEPTH + 1) // 2
N_ODD = DEPTH // 2
CONV_CH = D_MODEL // 2
CONV_WIDTH = 31
MLA_HEADS = 16
Q_LORA = 1024
KV_LORA = 512
QK_NOPE = 128
QK_ROPE = 64
V_HEAD = 128
ROPE_THETA = 10000.0
AB_IN = 2 * CONV_CH + Q_LORA + KV_LORA + QK_ROPE
AB_OUT = CONV_CH + MLA_HEADS * V_HEAD
C_HEADS = 32
C_HEAD_DIM = D_MODEL // C_HEADS
BAND_CHUNKS = 8
BAND_PAST = BAND_CHUNKS * CHUNK
MAX_REL = 128
MEM_LEN = 256
MEM_HEADS = 4
MEM_HEAD_DIM = 256
D_FF = ((8 * D_MODEL // 3 + 255) // 256) * 256

Q_BLOCK = 128
EPS = 1e-6
NEG = -1e30

kernel_name = 'hybrid_streaming_encoder_step'


def _rms(x, g):
    x32 = x.astype(jnp.float32)
    y = x32 * lax.rsqrt(jnp.mean(x32 * x32, axis=-1, keepdims=True) + EPS)
    return (y * g.astype(jnp.float32)).astype(x.dtype)


def _layernorm(x, g, b):
    x32 = x.astype(jnp.float32)
    mu = jnp.mean(x32, axis=-1, keepdims=True)
    var = jnp.mean(jnp.square(x32 - mu), axis=-1, keepdims=True)
    y = (x32 - mu) * lax.rsqrt(var + EPS)
    return (y * g.astype(jnp.float32) + b.astype(jnp.float32)).astype(x.dtype)


def _rope_tables(pos):
    inv = ROPE_THETA ** (-jnp.arange(0, QK_ROPE, 2, dtype=jnp.float32) / QK_ROPE)
    ang = pos.astype(jnp.float32)[:, None] * inv[None, :]
    return jnp.cos(ang), jnp.sin(ang)


def _rope(x, cos, sin):
    x32 = x.astype(jnp.float32)
    half = x.shape[-1] // 2
    x1, x2 = x32[..., :half], x32[..., half:]
    return jnp.concatenate([x1 * cos - x2 * sin, x1 * sin + x2 * cos], axis=-1).astype(x.dtype)


def _conv_module(u, hist, conv_w, conv_b, ln_g, ln_b):
    xp = jnp.concatenate([hist, u], axis=1)
    y = lax.conv_general_dilated(xp, conv_w[:, None, :], window_strides=(1,), padding='VALID',
                                 dimension_numbers=('NWC', 'WIO', 'NWC'),
                                 feature_group_count=CONV_CH) + conv_b
    y = jax.nn.silu(_layernorm(y, ln_g, ln_b))
    return y, xp[:, xp.shape[1] - (CONV_WIDTH - 1):]


def _blocked_chunk_causal_attn(q_nope, q_pe, k_nope, k_pe, v, q_pos, k_pos):
    B, T, H, _ = q_nope.shape
    qb = min(Q_BLOCK, T)
    nb = T // qb
    scale = (QK_NOPE + QK_ROPE) ** -0.5
    k_chunk = k_pos // CHUNK

    def blocks(a):
        return jnp.moveaxis(a.reshape((B, nb, qb) + a.shape[2:]), 1, 0)

    def one(args):
        qn, qp, pos = args
        s = (jnp.einsum('bqhd,bkhd->bhqk', qn, k_nope)
             + jnp.einsum('bqhr,bkr->bhqk', qp, k_pe)).astype(jnp.float32) * scale
        mask = k_chunk[None, :] <= (pos // CHUNK)[:, None]
        s = jnp.where(mask[None, None], s, NEG)
        p = jax.nn.softmax(s, axis=-1).astype(v.dtype)
        return jnp.einsum('bhqk,bkhd->bqhd', p, v)

    o = lax.map(one, (blocks(q_nope), blocks(q_pe), q_pos.reshape(nb, qb)))
    return jnp.moveaxis(o, 0, 1).reshape(B, T, H, v.shape[-1])


def _mla(cq, ckv, kpe_raw, cos, sin, q_pos, lat_hist, kpe_hist, q_norm_g, w_uq, kv_norm_g, w_ukv):
    B, T, _ = cq.shape
    q = (_rms(cq, q_norm_g) @ w_uq).reshape(B, T, MLA_HEADS, QK_NOPE + QK_ROPE)
    q_nope = q[..., :QK_NOPE]
    q_pe = _rope(q[..., QK_NOPE:], cos[None, :, None], sin[None, :, None])
    lat = _rms(ckv, kv_norm_g)
    kpe = _rope(kpe_raw, cos[None], sin[None])
    lat_all = lat if lat_hist is None else jnp.concatenate([lat_hist, lat], axis=1)
    kpe_all = kpe if kpe_hist is None else jnp.concatenate([kpe_hist, kpe], axis=1)
    L = lat_all.shape[1]
    kv = (lat_all @ w_ukv).reshape(B, L, MLA_HEADS, QK_NOPE + V_HEAD)
    k_nope, v = kv[..., :QK_NOPE], kv[..., QK_NOPE:]
    k_pos = jnp.arange(L, dtype=jnp.int32)
    o = _blocked_chunk_causal_attn(q_nope, q_pe, k_nope, kpe_all, v, q_pos, k_pos)
    return o.reshape(B, T, MLA_HEADS * V_HEAD), lat, kpe


def _band_block(q, k, v, q_pos, k_pos, table):
    s = jnp.einsum('bqhd,bkhd->bhqk', q, k).astype(jnp.float32) * (C_HEAD_DIM ** -0.5)
    rel = jnp.clip(q_pos[:, None] - k_pos[None, :], -MAX_REL, MAX_REL) + MAX_REL
    s = s + table[:, rel].astype(jnp.float32)[None]
    qc = (q_pos // CHUNK)[:, None]
    kc = (k_pos // CHUNK)[None, :]
    mask = (kc <= qc) & (kc >= qc - BAND_CHUNKS) & (k_pos[None, :] >= 0)
    s = jnp.where(mask[None, None], s, NEG)
    p = jax.nn.softmax(s, axis=-1).astype(v.dtype)
    return jnp.einsum('bhqk,bkhd->bqhd', p, v)


def _band_prompt(q, k, v, table):
    B, T, H, Dh = q.shape
    n_chunks = T // CHUNK
    band = BAND_PAST + CHUNK
    pad = ((0, 0), (BAND_PAST, 0), (0, 0), (0, 0))
    kp, vp = jnp.pad(k, pad), jnp.pad(v, pad)

    def one(c):
        start = c * CHUNK
        qc = lax.dynamic_slice_in_dim(q, start, CHUNK, axis=1)
        kb = lax.dynamic_slice_in_dim(kp, start, band, axis=1)
        vb = lax.dynamic_slice_in_dim(vp, start, band, axis=1)
        q_pos = start + jnp.arange(CHUNK, dtype=jnp.int32)
        k_pos = start - BAND_PAST + jnp.arange(band, dtype=jnp.int32)
        return _band_block(qc, kb, vb, q_pos, k_pos, table)

    o = lax.map(one, jnp.arange(n_chunks, dtype=jnp.int32))
    return jnp.moveaxis(o, 0, 1).reshape(B, T, H * Dh)


def _band_step(q, k, v, k_hist, v_hist, P, table):
    B, T, H, Dh = q.shape
    Lh = k_hist.shape[1]
    kb = jnp.concatenate([k_hist, k], axis=1)
    vb = jnp.concatenate([v_hist, v], axis=1)
    q_pos = P + jnp.arange(T, dtype=jnp.int32)
    k_pos = (P - Lh) + jnp.arange(Lh + T, dtype=jnp.int32)
    return _band_block(q, kb, vb, q_pos, k_pos, table).reshape(B, T, H * Dh)


def _mem_kv(mem, g, w_kv):
    B, M, _ = mem.shape
    kv = (_rms(mem, g) @ w_kv).reshape(B, M, 2, MEM_HEADS, MEM_HEAD_DIM)
    return kv[:, :, 0], kv[:, :, 1]


def _mem_attn(h, mk, mv, w_q, w_o):
    B, T, _ = h.shape
    q = (h @ w_q).reshape(B, T, MEM_HEADS, MEM_HEAD_DIM)
    s = jnp.einsum('bqhd,bmhd->bhqm', q, mk).astype(jnp.float32) * (MEM_HEAD_DIM ** -0.5)
    p = jax.nn.softmax(s, axis=-1).astype(mv.dtype)
    o = jnp.einsum('bhqm,bmhd->bqhd', p, mv).reshape(B, T, MEM_HEADS * MEM_HEAD_DIM)
    return o @ w_o


def _swiglu(h, w_gate_up, w_down):
    gu = h @ w_gate_up
    return (jax.nn.silu(gu[..., :D_FF]) * gu[..., D_FF:]) @ w_down


def _trunk(x, P, mem, mem_k_c, mem_v_c, lat_c, kpe_c, conv_c, bk_c, bv_c, W):
    prompt = mem is not None
    B, T, _ = x.shape
    q_pos = P + jnp.arange(T, dtype=jnp.int32)
    cos, sin = _rope_tables(q_pos)
    lats, kpes, convs, bks, bvs, mks, mvs = [], [], [], [], [], [], []
    s1, s2 = CONV_CH, 2 * CONV_CH
    s3, s4 = s2 + Q_LORA, s2 + Q_LORA + KV_LORA
    for l in range(DEPTH):
        h = _rms(x, W['norm_mix_pre'][l])
        if l % 2 == 0:
            e = l // 2
            z = h @ W['w_in_ab'][e]
            u = z[..., :s1] * jax.nn.sigmoid(z[..., s1:s2])
            hist = jnp.zeros((B, CONV_WIDTH - 1, CONV_CH), u.dtype) if prompt else conv_c[e]
            ya, conv_new = _conv_module(u, hist, W['conv_w'][e], W['conv_b'][e],
                                        W['conv_ln_g'][e], W['conv_ln_b'][e])
            yb, lat, kpe = _mla(z[..., s2:s3], z[..., s3:s4], z[..., s4:], cos, sin, q_pos,
                                None if prompt else lat_c[e], None if prompt else kpe_c[e],
                                W['q_norm_g'][e], W['w_uq'][e], W['kv_norm_g'][e], W['w_ukv'][e])
            y = jnp.concatenate([ya, yb], axis=-1) @ W['w_out_ab'][e]
            lats.append(lat)
            kpes.append(kpe)
            convs.append(conv_new)
        else:
            o = l // 2
            qkv = (h @ W['w_qkv_c'][o]).reshape(B, T, 3, C_HEADS, C_HEAD_DIM)
            q, k, v = qkv[:, :, 0], qkv[:, :, 1], qkv[:, :, 2]
            if prompt:
                yc = _band_prompt(q, k, v, W['rel_bias_c'][o])
                keep = min(BAND_PAST, T)
                bks.append(k[:, T - keep:])
                bvs.append(v[:, T - keep:])
            else:
                yc = _band_step(q, k, v, bk_c[o], bv_c[o], P, W['rel_bias_c'][o])
                bks.append(k)
                bvs.append(v)
            y = yc @ W['w_out_c'][o]
        x = x + _rms(y, W['norm_mix_post'][l])
        if prompt:
            mk, mv = _mem_kv(mem, W['norm_memkv'][l], W['w_kv_mem'][l])
            mks.append(mk)
            mvs.append(mv)
        else:
            mk, mv = mem_k_c[l], mem_v_c[l]
        h = _rms(x, W['norm_mem_pre'][l])
        x = x + _rms(_mem_attn(h, mk, mv, W['w_q_mem'][l], W['w_o_mem'][l]), W['norm_mem_post'][l])
        h = _rms(x, W['norm_ffn_pre'][l])
        x = x + _rms(_swiglu(h, W['w_gate_up'][l], W['w_down'][l]), W['norm_ffn_post'][l])
    new_mk = jnp.stack(mks, 0) if prompt else None
    new_mv = jnp.stack(mvs, 0) if prompt else None
    return (x, jnp.stack(lats, 0), jnp.stack(kpes, 0), jnp.stack(convs, 0),
            jnp.stack(bks, 0), jnp.stack(bvs, 0), new_mk, new_mv)


def setup_inputs(seed: int = 0) -> dict:
    key = jax.random.key(seed)
    ks = iter(jax.random.split(key, 48))

    def nrm(shape, scale=1.0):
        return jax.random.normal(next(ks), shape, jnp.float32) * scale

    def gain(shape):
        return 1.0 + 0.05 * nrm(shape)

    lc = min(BAND_PAST, PAST_LEN)
    return {
        'x_prompt': nrm((BATCH, SEQ, D_MODEL)),
        'x_sample': nrm((DEC_BATCH, DEC_SEQ, D_MODEL)),
        'cache_mla_latent': nrm((N_EVEN, DEC_BATCH, PAST_LEN, KV_LORA)),
        'cache_mla_kpe': nrm((N_EVEN, DEC_BATCH, PAST_LEN, QK_ROPE)),
        'state_conv': nrm((N_EVEN, DEC_BATCH, CONV_WIDTH - 1, CONV_CH), 0.5),
        'cache_band_k': nrm((N_ODD, DEC_BATCH, lc, C_HEADS, C_HEAD_DIM)),
        'cache_band_v': nrm((N_ODD, DEC_BATCH, lc, C_HEADS, C_HEAD_DIM)),
        'cache_mem_k': nrm((DEPTH, DEC_BATCH, MEM_LEN, MEM_HEADS, MEM_HEAD_DIM)),
        'cache_mem_v': nrm((DEPTH, DEC_BATCH, MEM_LEN, MEM_HEADS, MEM_HEAD_DIM)),
        'mem_prompt': nrm((BATCH, MEM_LEN, D_MODEL)),
        'norm_mix_pre': gain((DEPTH, D_MODEL)),
        'norm_mix_post': gain((DEPTH, D_MODEL)),
        'norm_mem_pre': gain((DEPTH, D_MODEL)),
        'norm_mem_post': gain((DEPTH, D_MODEL)),
        'norm_memkv': gain((DEPTH, D_MODEL)),
        'norm_ffn_pre': gain((DEPTH, D_MODEL)),
        'norm_ffn_post': gain((DEPTH, D_MODEL)),
        'w_in_ab': nrm((N_EVEN, D_MODEL, AB_IN), D_MODEL ** -0.5),
        'conv_w': nrm((N_EVEN, CONV_WIDTH, CONV_CH), CONV_WIDTH ** -0.5),
        'conv_b': nrm((N_EVEN, CONV_CH), 0.02),
        'conv_ln_g': gain((N_EVEN, CONV_CH)),
        'conv_ln_b': nrm((N_EVEN, CONV_CH), 0.02),
        'q_norm_g': gain((N_EVEN, Q_LORA)),
        'w_uq': nrm((N_EVEN, Q_LORA, MLA_HEADS * (QK_NOPE + QK_ROPE)), Q_LORA ** -0.5),
        'kv_norm_g': gain((N_EVEN, KV_LORA)),
        'w_ukv': nrm((N_EVEN, KV_LORA, MLA_HEADS * (QK_NOPE + V_HEAD)), KV_LORA ** -0.5),
        'w_out_ab': nrm((N_EVEN, AB_OUT, D_MODEL), AB_OUT ** -0.5),
        'w_qkv_c': nrm((N_ODD, D_MODEL, 3 * C_HEADS * C_HEAD_DIM), D_MODEL ** -0.5),
        'rel_bias_c': nrm((N_ODD, C_HEADS, 2 * MAX_REL + 1), 0.1),
        'w_out_c': nrm((N_ODD, C_HEADS * C_HEAD_DIM, D_MODEL), (C_HEADS * C_HEAD_DIM) ** -0.5),
        'w_q_mem': nrm((DEPTH, D_MODEL, MEM_HEADS * MEM_HEAD_DIM), D_MODEL ** -0.5),
        'w_kv_mem': nrm((DEPTH, D_MODEL, 2 * MEM_HEADS * MEM_HEAD_DIM), D_MODEL ** -0.5),
        'w_o_mem': nrm((DEPTH, MEM_HEADS * MEM_HEAD_DIM, D_MODEL), (MEM_HEADS * MEM_HEAD_DIM) ** -0.5),
        'w_gate_up': nrm((DEPTH, D_MODEL, 2 * D_FF), D_MODEL ** -0.5),
        'w_down': nrm((DEPTH, D_FF, D_MODEL), D_FF ** -0.5),
    }


def reference(x_prompt, x_sample, cache_mla_latent, cache_mla_kpe, state_conv, cache_band_k,
              cache_band_v, cache_mem_k, cache_mem_v, mem_prompt, norm_mix_pre, norm_mix_post,
              norm_mem_pre, norm_mem_post, norm_memkv, norm_ffn_pre, norm_ffn_post, w_in_ab,
              conv_w, conv_b, conv_ln_g, conv_ln_b, q_norm_g, w_uq, kv_norm_g, w_ukv, w_out_ab,
              w_qkv_c, rel_bias_c, w_out_c, w_q_mem, w_kv_mem, w_o_mem, w_gate_up, w_down):
    W = dict(norm_mix_pre=norm_mix_pre, norm_mix_post=norm_mix_post, norm_mem_pre=norm_mem_pre,
             norm_mem_post=norm_mem_post, norm_memkv=norm_memkv, norm_ffn_pre=norm_ffn_pre,
             norm_ffn_post=norm_ffn_post, w_in_ab=w_in_ab, conv_w=conv_w, conv_b=conv_b,
             conv_ln_g=conv_ln_g, conv_ln_b=conv_ln_b, q_norm_g=q_norm_g, w_uq=w_uq,
             kv_norm_g=kv_norm_g, w_ukv=w_ukv, w_out_ab=w_out_ab, w_qkv_c=w_qkv_c,
             rel_bias_c=rel_bias_c, w_out_c=w_out_c, w_q_mem=w_q_mem, w_kv_mem=w_kv_mem,
             w_o_mem=w_o_mem, w_gate_up=w_gate_up, w_down=w_down)
    y_p, lat_p, kpe_p, conv_p, bk_p, bv_p, mk_p, mv_p = _trunk(
        x_prompt, 0, mem_prompt, None, None, None, None, None, None, None, W)
    past = cache_mla_latent.shape[2]
    y_s, lat_s, kpe_s, conv_s, bk_s, bv_s, _, _ = _trunk(
        x_sample, past, None, cache_mem_k, cache_mem_v, cache_mla_latent, cache_mla_kpe,
        state_conv, cache_band_k, cache_band_v, W)
    return (y_p, y_s, lat_p, kpe_p, conv_p, bk_p, bv_p, mk_p, mv_p, lat_s, kpe_s, conv_s, bk_s, bv_s)
```

```python
import functools

import jax
import jax.numpy as jnp
from jax import lax
from jax.experimental import pallas as pl
from jax.experimental.pallas import tpu as pltpu

F32 = jnp.float32
BF16 = jnp.bfloat16

EPS = 1e-6
NEG = -1e30
CHUNK = 64
BAND_CHUNKS = 8
ROPE_THETA = 10000.0

LANE = 128
MXU_DIM = 256
VMEM_LIMIT_BYTES = 56 * 1024 * 1024
FF_ALIGN = 1024

BAND_QB = 4 * CHUNK
BAND_KB = BAND_CHUNKS * CHUNK + BAND_QB


def _cparams(*sem):
    return pltpu.CompilerParams(dimension_semantics=sem, vmem_limit_bytes=VMEM_LIMIT_BYTES)


def _blk(n, pref):
    if n <= pref:
        return n
    b = pref
    while n % b:
        b //= 2
    assert b >= 8, (n, pref)
    return b


def _rms_f32(x, g):
    return x * lax.rsqrt(jnp.mean(x * x, axis=-1, keepdims=True) + EPS) * g


def _rmsnorm_kernel(x_ref, g_ref, o_ref):
    o_ref[...] = _rms_f32(x_ref[...].astype(F32), g_ref[...]).astype(o_ref.dtype)


def rmsnorm(x, g, out_dtype=BF16):
    m, d = x.shape
    bm = _blk(m, 256)
    return pl.pallas_call(
        _rmsnorm_kernel,
        grid=(m // bm,),
        in_specs=[pl.BlockSpec((bm, d), lambda i: (i, 0)), pl.BlockSpec((1, d), lambda i: (0, 0))],
        out_specs=pl.BlockSpec((bm, d), lambda i: (i, 0)),
        out_shape=jax.ShapeDtypeStruct((m, d), out_dtype),
        compiler_params=_cparams("parallel"),
        name="rmsnorm",
    )(x, g.reshape(1, d))


def _resnorm_kernel(x_ref, y_ref, gp_ref, gn_ref, xo_ref, h_ref):
    xn = x_ref[...] + _rms_f32(y_ref[...], gp_ref[...])
    xo_ref[...] = xn
    h_ref[...] = _rms_f32(xn, gn_ref[...]).astype(h_ref.dtype)


def _resnorm_last_kernel(x_ref, y_ref, gp_ref, xo_ref):
    xo_ref[...] = x_ref[...] + _rms_f32(y_ref[...], gp_ref[...])


def resnorm(x, y, g_post, g_next):
    m, d = x.shape
    bm = _blk(m, 256)
    row = pl.BlockSpec((bm, d), lambda i: (i, 0))
    vec = pl.BlockSpec((1, d), lambda i: (0, 0))
    if g_next is None:
        return pl.pallas_call(
            _resnorm_last_kernel, grid=(m // bm,), in_specs=[row, row, vec], out_specs=row,
            out_shape=jax.ShapeDtypeStruct((m, d), F32), compiler_params=_cparams("parallel"),
            name="resnorm_last",
        )(x, y, g_post.reshape(1, d)), None
    return pl.pallas_call(
        _resnorm_kernel, grid=(m // bm,), in_specs=[row, row, vec, vec], out_specs=[row, row],
        out_shape=[jax.ShapeDtypeStruct((m, d), F32), jax.ShapeDtypeStruct((m, d), BF16)],
        compiler_params=_cparams("parallel"), name="resnorm",
    )(x, y, g_post.reshape(1, d), g_next.reshape(1, d))


def _mm_kernel(a_ref, w_ref, o_ref):
    o_ref[...] = jnp.dot(a_ref[...], w_ref[...], preferred_element_type=F32).astype(o_ref.dtype)


def matmul(a, w, out_dtype, *, n_off=0, n=None, bm_pref=1024, bn_pref=1024):
    m, k = a.shape
    n = w.shape[1] - n_off if n is None else n
    bm = _blk(m, bm_pref)
    bn = _blk(n, bn_pref)
    while n_off % bn:
        bn //= 2
    assert bn % LANE == 0 or bn == w.shape[1]
    joff = n_off // bn
    return pl.pallas_call(
        _mm_kernel,
        grid=(m // bm, n // bn),
        in_specs=[pl.BlockSpec((bm, k), lambda i, j: (i, 0)),
                  pl.BlockSpec((k, bn), lambda i, j: (0, j + joff))],
        out_specs=pl.BlockSpec((bm, bn), lambda i, j: (i, j)),
        out_shape=jax.ShapeDtypeStruct((m, n), out_dtype),
        compiler_params=_cparams("parallel", "parallel"),
        name="matmul",
    )(a, w)


def _mm2_kernel(a1_ref, a2_ref, w1_ref, w2_ref, o_ref):
    acc = jnp.dot(a1_ref[...], w1_ref[...], preferred_element_type=F32)
    acc = acc + jnp.dot(a2_ref[...], w2_ref[...], preferred_element_type=F32)
    o_ref[...] = acc.astype(o_ref.dtype)


def matmul_cat2(a1, a2, w, out_dtype):
    m, k1 = a1.shape
    assert a2.shape == (m, k1) and w.shape[0] == 2 * k1
    n = w.shape[1]
    bm = _blk(m, 1024)
    bn = _blk(n, 1024)
    return pl.pallas_call(
        _mm2_kernel,
        grid=(m // bm, n // bn),
        in_specs=[pl.BlockSpec((bm, k1), lambda i, j: (i, 0)),
                  pl.BlockSpec((bm, k1), lambda i, j: (i, 0)),
                  pl.BlockSpec((k1, bn), lambda i, j: (0, j)),
                  pl.BlockSpec((k1, bn), lambda i, j: (1, j))],
        out_specs=pl.BlockSpec((bm, bn), lambda i, j: (i, j)),
        out_shape=jax.ShapeDtypeStruct((m, n), out_dtype),
        compiler_params=_cparams("parallel", "parallel"),
        name="matmul_cat2",
    )(a1, a2, w, w)


def _mm_acc_kernel(a_ref, w_ref, o_ref, acc_ref, *, nk):
    kk = pl.program_id(2)
    part = jnp.dot(a_ref[...], w_ref[...], preferred_element_type=F32)

    @pl.when(kk == 0)
    def _():
        acc_ref[...] = part

    @pl.when(kk > 0)
    def _():
        acc_ref[...] += part

    @pl.when(kk == nk - 1)
    def _():
        o_ref[...] = acc_ref[...].astype(o_ref.dtype)


def matmul_ktiled(a, w, out_dtype, *, bk):
    m, k = a.shape
    n = w.shape[1]
    bm = _blk(m, 1024)
    bn = _blk(n, 1024)
    assert k % bk == 0
    nk = k // bk
    return pl.pallas_call(
        functools.partial(_mm_acc_kernel, nk=nk),
        grid=(m // bm, n // bn, nk),
        in_specs=[pl.BlockSpec((bm, bk), lambda i, j, kk: (i, kk)),
                  pl.BlockSpec((bk, bn), lambda i, j, kk: (kk, j))],
        out_specs=pl.BlockSpec((bm, bn), lambda i, j, kk: (i, j)),
        out_shape=jax.ShapeDtypeStruct((m, n), out_dtype),
        scratch_shapes=[pltpu.VMEM((bm, bn), F32)],
        compiler_params=_cparams("parallel", "parallel", "arbitrary"),
        name="matmul_ktiled",
    )(a, w)


def _swiglu_kernel(a_ref, wg_ref, wu_ref, o_ref):
    a = a_ref[...]
    g = jnp.dot(a, wg_ref[...], preferred_element_type=F32)
    u = jnp.dot(a, wu_ref[...], preferred_element_type=F32)
    o_ref[...] = (g * jax.nn.sigmoid(g) * u).astype(o_ref.dtype)


def swiglu_up(h, w_gu):
    m, k = h.shape
    ff = w_gu.shape[1] // 2
    bm = _blk(m, 1024)
    bn = _blk(ff, 512)
    nj = ff // bn
    return pl.pallas_call(
        _swiglu_kernel,
        grid=(m // bm, nj),
        in_specs=[pl.BlockSpec((bm, k), lambda i, j: (i, 0)),
                  pl.BlockSpec((k, bn), lambda i, j: (0, j)),
                  pl.BlockSpec((k, bn), lambda i, j: (0, j + nj))],
        out_specs=pl.BlockSpec((bm, bn), lambda i, j: (i, j)),
        out_shape=jax.ShapeDtypeStruct((m, ff), BF16),
        compiler_params=_cparams("parallel", "parallel"),
        name="swiglu_up",
    )(h, w_gu, w_gu)


def _conv_kernel(a_ref, b_ref, ah_ref, bh_ref, hist_ref, cw_ref, cb_ref, lg_ref, lb_ref, o_ref,
                 buf_ref, y_ref, *, tb, width, halo, tt):
    t_blk = pl.program_id(1)
    a = a_ref[...].astype(F32)
    u_cur = a * jax.nn.sigmoid(b_ref[...].astype(F32))
    @pl.when(t_blk == 0)
    def _():
        buf_ref[0:halo, :] = hist_ref[...]

    @pl.when(t_blk > 0)
    def _():
        ah = ah_ref[...].astype(F32)
        buf_ref[0:halo, :] = ah * jax.nn.sigmoid(bh_ref[...].astype(F32))

    buf_ref[halo:halo + tb, :] = u_cur
    ch = a.shape[1]
    base = halo - (width - 1)

    def lane_tile(ct, carry):
        c0 = pl.multiple_of(ct * LANE, LANE)
        bias = cb_ref[:, pl.ds(c0, LANE)]
        for t0 in range(0, tb, tt):
            acc = jnp.broadcast_to(bias, (tt, LANE))
            for w in range(width):
                acc = acc + cw_ref[w:w + 1, pl.ds(c0, LANE)] * buf_ref[pl.ds(base + t0 + w, tt), pl.ds(c0, LANE)]
            y_ref[pl.ds(t0, tt), pl.ds(c0, LANE)] = acc
        return carry

    lax.fori_loop(0, ch // LANE, lane_tile, 0)
    y = y_ref[...]
    mu = jnp.mean(y, axis=-1, keepdims=True)
    yc = y - mu
    var = jnp.mean(yc * yc, axis=-1, keepdims=True)
    z = yc * lax.rsqrt(var + EPS) * lg_ref[...] + lb_ref[...]
    o_ref[...] = (z * jax.nn.sigmoid(z)).astype(o_ref.dtype)


def conv_module(z_main, hist, conv_w, conv_b, ln_g, ln_b, *, nb, t, ch):
    width = conv_w.shape[0]
    halo = hist.shape[1]
    tb = _blk(t, 256)
    tt = _blk(tb, 64)
    nt = t // tb
    hb = tb // halo
    cw = jnp.pad(conv_w, ((0, halo - width), (0, 0)))
    row = lambda c: pl.BlockSpec((tb, ch), lambda b, i: (b * nt + i, c))
    prev = lambda c: pl.BlockSpec((halo, ch), lambda b, i: (jnp.maximum((b * nt + i) * hb - 1, 0), c))
    vec = pl.BlockSpec((1, ch), lambda b, i: (0, 0))
    return pl.pallas_call(
        functools.partial(_conv_kernel, tb=tb, width=width, halo=halo, tt=tt),
        grid=(nb, nt),
        in_specs=[row(0), row(1), prev(0), prev(1),
                  pl.BlockSpec((None, halo, ch), lambda b, i: (b, 0, 0)),
                  pl.BlockSpec((halo, ch), lambda b, i: (0, 0)), vec, vec, vec],
        out_specs=pl.BlockSpec((tb, ch), lambda b, i: (b * nt + i, 0)),
        out_shape=jax.ShapeDtypeStruct((nb * t, ch), BF16),
        scratch_shapes=[pltpu.VMEM((halo + tb, ch), F32), pltpu.VMEM((tb, ch), F32)],
        compiler_params=_cparams("parallel", "parallel"),
        name="conv_module",
    )(z_main, z_main, z_main, z_main, hist, cw, conv_b.reshape(1, ch), ln_g.reshape(1, ch),
      ln_b.reshape(1, ch))


def _glu_tail_kernel(z_ref, o_ref, *, ch, keep):
    z = z_ref[...]
    u = z[:, :ch] * jax.nn.sigmoid(z[:, ch:])
    o_ref[...] = u[u.shape[0] - keep:, :]


def glu_tail(z_tail, *, nb, rows, ch, keep):
    return pl.pallas_call(
        functools.partial(_glu_tail_kernel, ch=ch, keep=keep),
        grid=(nb,),
        in_specs=[pl.BlockSpec((rows, 2 * ch), lambda b: (b, 0))],
        out_specs=pl.BlockSpec((None, keep, ch), lambda b: (b, 0, 0)),
        out_shape=jax.ShapeDtypeStruct((nb, keep, ch), F32),
        compiler_params=_cparams("parallel"),
        name="glu_tail",
    )(z_tail)


def _rope_pairs(v, cc, ss):
    return v * cc + pltpu.roll(v, LANE // 2, axis=1) * ss


def _mla_kv_prep_kernel(z_ref, g_ref, cc_ref, ss_ref, lat_ref, kpe_ref, a_ref, *, kvl):
    z = z_ref[...]
    lat = _rms_f32(z[:, :kvl], g_ref[...])
    rot = _rope_pairs(z[:, kvl:kvl + LANE], cc_ref[...], ss_ref[...])
    lat_ref[...] = lat
    kpe_ref[...] = rot[:, :LANE // 2]
    a_ref[:, :kvl] = lat.astype(a_ref.dtype)
    a_ref[:, kvl:] = rot.astype(a_ref.dtype)


def mla_kv_prep(z_kv, g, cc, ss, *, t):
    m, wdt = z_kv.shape
    kvl = wdt - LANE
    bm = _blk(t, 256)
    nt = t // bm
    row = lambda wd: pl.BlockSpec((bm, wd), lambda i: (i, 0))
    tab = pl.BlockSpec((bm, LANE), lambda i: (i % nt, 0))
    return pl.pallas_call(
        functools.partial(_mla_kv_prep_kernel, kvl=kvl),
        grid=(m // bm,),
        in_specs=[row(wdt), pl.BlockSpec((1, kvl), lambda i: (0, 0)), tab, tab],
        out_specs=[row(kvl), row(LANE // 2), row(wdt)],
        out_shape=[jax.ShapeDtypeStruct((m, kvl), F32), jax.ShapeDtypeStruct((m, LANE // 2), F32),
                   jax.ShapeDtypeStruct((m, wdt), BF16)],
        compiler_params=_cparams("parallel"),
        name="mla_kv_prep",
    )(z_kv, g.reshape(1, kvl), cc, ss)


def _mla_q_kernel(z_ref, g_ref, w_ref, cc_ref, ss_ref, o_ref, *, heads, scale):
    a = _rms_f32(z_ref[...].astype(F32), g_ref[...]).astype(BF16)
    acc = jnp.dot(a, w_ref[...], preferred_element_type=F32)
    cc = cc_ref[...]
    ss = ss_ref[...]
    for h in range(heads):
        c0 = h * MXU_DIM
        o_ref[:, c0:c0 + LANE] = (acc[:, c0:c0 + LANE] * scale).astype(o_ref.dtype)
        rot = _rope_pairs(acc[:, c0 + LANE:c0 + MXU_DIM], cc, ss)
        o_ref[:, c0 + LANE:c0 + MXU_DIM] = (rot * scale).astype(o_ref.dtype)


def mla_q_proj(z_main, col_blk, g, w_uq_ext, cc, ss, *, t, heads, scale):
    m = z_main.shape[0]
    ql, n = w_uq_ext.shape
    bm = _blk(t, 512)
    nt = t // bm
    tab = pl.BlockSpec((bm, LANE), lambda i: (i % nt, 0))
    return pl.pallas_call(
        functools.partial(_mla_q_kernel, heads=heads, scale=scale),
        grid=(m // bm,),
        in_specs=[pl.BlockSpec((bm, ql), lambda i: (i, col_blk)),
                  pl.BlockSpec((1, ql), lambda i: (0, 0)),
                  pl.BlockSpec((ql, n), lambda i: (0, 0)), tab, tab],
        out_specs=pl.BlockSpec((bm, n), lambda i: (i, 0)),
        out_shape=jax.ShapeDtypeStruct((m, n), BF16),
        compiler_params=_cparams("parallel"),
        name="mla_q_proj",
    )(z_main, g.reshape(1, ql), w_uq_ext, cc, ss)


def _mla_attn_kernel(q_ref, k_ref, v_ref, o_ref, m_ref, l_ref, acc_ref, *, heads, dv, bq, bk, nk,
                     mode, kv_len):
    i = pl.program_id(1)
    j = pl.program_id(2)

    @pl.when(j == 0)
    def _():
        m_ref[...] = jnp.full(m_ref.shape, NEG, F32)
        l_ref[...] = jnp.zeros(l_ref.shape, F32)
        acc_ref[...] = jnp.zeros(acc_ref.shape, F32)

    def step(mask):
        for h in range(heads):
            qh = q_ref[:, h * MXU_DIM:(h + 1) * MXU_DIM]
            kh = k_ref[:, h * MXU_DIM:(h + 1) * MXU_DIM]
            s = lax.dot_general(qh, kh, (((1,), (1,)), ((), ())), preferred_element_type=F32)
            if mask is not None:
                s = jnp.where(mask, s, NEG)
            m_prev = m_ref[h]
            m_new = jnp.maximum(m_prev, jnp.max(s, axis=1, keepdims=True))
            alpha = jnp.exp(m_prev - m_new)
            p = jnp.exp(s - m_new[:, :1])
            l_ref[h] = alpha * l_ref[h] + jnp.sum(p, axis=1, keepdims=True)
            pv = jnp.dot(p.astype(BF16), v_ref[:, h * dv:(h + 1) * dv], preferred_element_type=F32)
            acc_ref[:, h * dv:(h + 1) * dv] = alpha * acc_ref[:, h * dv:(h + 1) * dv] + pv
            m_ref[h] = m_new

    if mode == "block_causal":
        @pl.when(j < i)
        def _():
            step(None)

        @pl.when(j == i)
        def _():
            qc = lax.broadcasted_iota(jnp.int32, (bq, bk), 0) // CHUNK
            kc = lax.broadcasted_iota(jnp.int32, (bq, bk), 1) // CHUNK
            step(kc <= qc)
    else:
        @pl.when(j < nk - 1)
        def _():
            step(None)

        @pl.when(j == nk - 1)
        def _():
            kpos = (nk - 1) * bk + lax.broadcasted_iota(jnp.int32, (bq, bk), 1)
            step(kpos < kv_len if kv_len < nk * bk else None)

    @pl.when(j == nk - 1)
    def _():
        for h in range(heads):
            o_ref[:, h * dv:(h + 1) * dv] = (acc_ref[:, h * dv:(h + 1) * dv] / l_ref[h]).astype(o_ref.dtype)


def mla_attention(q, kv, *, nb, tq, lk, kv_len, heads, dv, mode):
    kw = heads * MXU_DIM
    vw = heads * dv
    assert kw % vw == 0 and dv == LANE
    bq = _blk(tq, 512)
    if mode == "block_causal":
        assert lk == tq and kv_len == lk and bq % CHUNK == 0
        bk = bq
    else:
        bk = _blk(lk, 768)
    nq, nk = tq // bq, lk // bk
    assert kv_len > (nk - 1) * bk, "key padding must sit in the last key block"
    if mode == "block_causal":
        kidx = lambda b, i, j: (b * nk + jnp.minimum(j, i), 0)
        vidx = lambda b, i, j: (b * nk + jnp.minimum(j, i), kw // vw)
    else:
        kidx = lambda b, i, j: (b * nk + j, 0)
        vidx = lambda b, i, j: (b * nk + j, kw // vw)
    return pl.pallas_call(
        functools.partial(_mla_attn_kernel, heads=heads, dv=dv, bq=bq, bk=bk, nk=nk, mode=mode,
                          kv_len=kv_len),
        grid=(nb, nq, nk),
        in_specs=[pl.BlockSpec((bq, kw), lambda b, i, j: (b * nq + i, 0)),
                  pl.BlockSpec((bk, kw), kidx),
                  pl.BlockSpec((bk, vw), vidx)],
        out_specs=pl.BlockSpec((bq, vw), lambda b, i, j: (b * nq + i, 0)),
        out_shape=jax.ShapeDtypeStruct((nb * tq, vw), BF16),
        scratch_shapes=[pltpu.VMEM((heads, bq, LANE), F32), pltpu.VMEM((heads, bq, LANE), F32),
                        pltpu.VMEM((bq, vw), F32)],
        compiler_params=_cparams("parallel", "parallel", "arbitrary"),
        name="mla_attention",
    )(q, kv, kv)


def _band_bias_kernel(t_ref, o_ref, *, rows, max_rel, kb, past):
    t = t_ref[...]
    hi = t.astype(BF16)
    r1 = t - hi.astype(F32)
    mid = r1.astype(BF16)
    lo = (r1 - mid.astype(F32)).astype(BF16)
    nrel = t.shape[1]
    j = lax.broadcasted_iota(jnp.int32, (nrel, kb), 1)
    r = lax.broadcasted_iota(jnp.int32, (nrel, kb), 0)
    kc = lax.broadcasted_iota(jnp.int32, (t.shape[0], kb), 1) // CHUNK - past // CHUNK
    for rr in range(rows):
        i = pl.program_id(0) * rows + rr
        idx = jnp.clip(i - j + past, -max_rel, max_rel) + max_rel
        onehot = jnp.where(r == idx, 1.0, 0.0).astype(BF16)
        vals = (jnp.dot(hi, onehot, preferred_element_type=F32)
                + jnp.dot(mid, onehot, preferred_element_type=F32)
                + jnp.dot(lo, onehot, preferred_element_type=F32))
        qc = i // CHUNK
        ok = (kc <= qc) & (kc >= qc - BAND_CHUNKS)
        o_ref[rr] = jnp.where(ok, vals, NEG)


def band_bias(table):
    heads, nrel = table.shape
    max_rel = (nrel - 1) // 2
    nrel_p = -(-nrel // LANE) * LANE
    tp = jnp.pad(table, ((0, 0), (0, nrel_p - nrel)))
    rows = 8
    past = BAND_CHUNKS * CHUNK
    return pl.pallas_call(
        functools.partial(_band_bias_kernel, rows=rows, max_rel=max_rel, kb=BAND_KB, past=past),
        grid=(BAND_QB // rows,),
        in_specs=[pl.BlockSpec((heads, nrel_p), lambda i: (0, 0))],
        out_specs=pl.BlockSpec((rows, heads, BAND_KB), lambda i: (i, 0, 0)),
        out_shape=jax.ShapeDtypeStruct((BAND_QB, heads, BAND_KB), F32),
        compiler_params=_cparams("parallel"),
        name="band_bias",
    )(tp)


def _softmax_pv(s, v_parts, widths):
    m = jnp.max(s, axis=1, keepdims=True)
    p = jnp.exp(s - m)
    l = jnp.sum(p, axis=1, keepdims=True)
    pb = p.astype(BF16)
    o = None
    c = 0
    for vp, wd in zip(v_parts, widths):
        part = jnp.dot(pb[:, c:c + wd], vp, preferred_element_type=F32)
        o = part if o is None else o + part
        c += wd
    return o / l


def _band_prompt_kernel(q_ref, k0_ref, k1_ref, k2_ref, v0_ref, v1_ref, v2_ref, bias_ref, o_ref, *,
                        group, dh, scale):
    qb = pl.program_id(2)
    col = lax.broadcasted_iota(jnp.int32, (BAND_QB, BAND_KB), 1)
    invalid = col < jnp.maximum(2 - qb, 0) * BAND_QB
    for g in range(group):
        sl = slice(g * dh, (g + 1) * dh)
        qh = q_ref[:, sl]
        parts = [lax.dot_general(qh, kr[:, sl], (((1,), (1,)), ((), ())), preferred_element_type=F32)
                 for kr in (k0_ref, k1_ref, k2_ref)]
        s = jnp.concatenate(parts, axis=1) * scale + bias_ref[g]
        s = jnp.where(invalid, NEG, s)
        o = _softmax_pv(s, [v0_ref[:, sl], v1_ref[:, sl], v2_ref[:, sl]], [BAND_QB] * 3)
        o_ref[:, sl] = o.astype(o_ref.dtype)


def band_attention_prompt(qkv, bias, *, nb, t, heads, dh):
    d = heads * dh
    group = min(heads, 8)
    ng = heads // group
    gw = group * dh
    nq = t // BAND_QB
    assert t % BAND_QB == 0 and BAND_KB == 3 * BAND_QB
    scale = dh ** -0.5

    def kspec(back, col0):
        return pl.BlockSpec((BAND_QB, gw),
                            lambda hg, b, i: (b * nq + jnp.maximum(i - back, 0), col0 + hg))

    return pl.pallas_call(
        functools.partial(_band_prompt_kernel, group=group, dh=dh, scale=scale),
        grid=(ng, nb, nq),
        in_specs=[pl.BlockSpec((BAND_QB, gw), lambda hg, b, i: (b * nq + i, hg)),
                  kspec(2, ng), kspec(1, ng), kspec(0, ng),
                  kspec(2, 2 * ng), kspec(1, 2 * ng), kspec(0, 2 * ng),
                  pl.BlockSpec((group, BAND_QB, BAND_KB), lambda hg, b, i: (hg, 0, 0))],
        out_specs=pl.BlockSpec((BAND_QB, gw), lambda hg, b, i: (b * nq + i, hg)),
        out_shape=jax.ShapeDtypeStruct((nb * t, d), BF16),
        compiler_params=_cparams("parallel", "parallel", "parallel"),
        name="band_attention_prompt",
    )(qkv, qkv, qkv, qkv, qkv, qkv, qkv, bias)


def _band_step_kernel(q_ref, k_ref, v_ref, bias_ref, o_ref, *, group, dh, scale):
    for g in range(group):
        sl = slice(g * dh, (g + 1) * dh)
        s = lax.dot_general(q_ref[:, sl], k_ref[:, sl], (((1,), (1,)), ((), ())),
                            preferred_element_type=F32)
        s = s * scale + bias_ref[g]
        o = _softmax_pv(s, [v_ref[:, sl]], [s.shape[1]])
        o_ref[:, sl] = o.astype(o_ref.dtype)


def band_attention_step(qkv, k_all, v_all, bias, *, nb, t, heads, dh):
    d = heads * dh
    group = min(heads, 8)
    ng = heads // group
    gw = group * dh
    lk = k_all.shape[0] // nb
    scale = dh ** -0.5
    return pl.pallas_call(
        functools.partial(_band_step_kernel, group=group, dh=dh, scale=scale),
        grid=(ng, nb),
        in_specs=[pl.BlockSpec((t, gw), lambda hg, b: (b, hg)),
                  pl.BlockSpec((lk, gw), lambda hg, b: (b, hg)),
                  pl.BlockSpec((lk, gw), lambda hg, b: (b, hg)),
                  pl.BlockSpec((group, t, lk), lambda hg, b: (hg, 0, 0))],
        out_specs=pl.BlockSpec((t, gw), lambda hg, b: (b, hg)),
        out_shape=jax.ShapeDtypeStruct((nb * t, d), BF16),
        compiler_params=_cparams("parallel", "parallel"),
        name="band_attention_step",
    )(qkv, k_all, v_all, bias)


def _mem_attn_kernel(q_ref, k_ref, v_ref, o_ref, *, heads, dh, scale):
    for h in range(heads):
        sl = slice(h * dh, (h + 1) * dh)
        s = lax.dot_general(q_ref[:, sl], k_ref[:, sl], (((1,), (1,)), ((), ())),
                            preferred_element_type=F32) * scale
        o = _softmax_pv(s, [v_ref[:, sl]], [s.shape[1]])
        o_ref[:, sl] = o.astype(o_ref.dtype)


def mem_attention(q, mk, mv, *, nb, t, heads, dh):
    d = heads * dh
    ml = mk.shape[0] // nb
    bq = _blk(t, 512)
    nq = t // bq
    return pl.pallas_call(
        functools.partial(_mem_attn_kernel, heads=heads, dh=dh, scale=dh ** -0.5),
        grid=(nb, nq),
        in_specs=[pl.BlockSpec((bq, d), lambda b, i: (b * nq + i, 0)),
                  pl.BlockSpec((ml, d), lambda b, i: (b, 0)),
                  pl.BlockSpec((ml, d), lambda b, i: (b, 0))],
        out_specs=pl.BlockSpec((bq, d), lambda b, i: (b * nq + i, 0)),
        out_shape=jax.ShapeDtypeStruct((nb * t, d), BF16),
        compiler_params=_cparams("parallel", "parallel"),
        name="mem_attention",
    )(q, mk, mv)


def _prep_weights(W, dims):
    heads, nope, rope, dv, kvl = dims["mla_heads"], dims["nope"], dims["rope"], dims["dv"], dims["kvl"]
    ch, ql = dims["ch"], dims["ql"]
    half = rope // 2
    P = {}
    w_in = W["w_in_ab"]
    s2 = 2 * ch + ql
    P["w_in_main"] = w_in[:, :, :s2].astype(BF16)
    kpe0 = s2 + kvl
    P["w_in_kv"] = jnp.concatenate(
        [w_in[:, :, s2:], w_in[:, :, kpe0 + half:kpe0 + rope], w_in[:, :, kpe0:kpe0 + half]],
        axis=-1).astype(BF16)
    ne = w_in.shape[0]
    wq = W["w_uq"].reshape(ne, ql, heads, nope + rope)
    pe = wq[..., nope:]
    pe_sw = jnp.concatenate([pe[..., half:], pe[..., :half]], axis=-1)
    P["w_uq"] = jnp.concatenate([wq[..., :nope], pe, pe_sw], axis=-1).reshape(
        ne, ql, heads * MXU_DIM).astype(BF16)
    wkv = W["w_ukv"].reshape(ne, kvl, heads, nope + dv)
    kslot = jnp.pad(wkv[..., :nope], ((0, 0), (0, LANE), (0, 0), (0, MXU_DIM - nope)))
    eye = jnp.zeros((LANE, MXU_DIM), F32).at[jnp.arange(rope), nope + jnp.arange(rope)].set(1.0)
    kslot = kslot + jnp.pad(eye, ((kvl, 0), (0, 0)))[None, :, None, :]
    vpart = jnp.pad(wkv[..., nope:], ((0, 0), (0, LANE), (0, 0), (0, 0)))
    P["w_ukv"] = jnp.concatenate(
        [kslot.reshape(ne, kvl + LANE, heads * MXU_DIM), vpart.reshape(ne, kvl + LANE, heads * dv)],
        axis=-1).astype(BF16)
    for name in ("w_out_ab", "w_qkv_c", "w_out_c", "w_q_mem", "w_kv_mem", "w_o_mem"):
        P[name] = W[name].astype(BF16)
    ff = W["w_down"].shape[1]
    ffp = -(-ff // FF_ALIGN) * FF_ALIGN
    wgu = W["w_gate_up"]
    P["w_gate_up"] = jnp.concatenate(
        [jnp.pad(wgu[:, :, :ff], ((0, 0), (0, 0), (0, ffp - ff))),
         jnp.pad(wgu[:, :, ff:], ((0, 0), (0, 0), (0, ffp - ff)))], axis=-1).astype(BF16)
    P["w_down"] = jnp.pad(W["w_down"], ((0, 0), (0, ffp - ff), (0, 0))).astype(BF16)
    P["ffp"] = ffp
    return P


def _rope_tabs(pos, rope):
    half = rope // 2
    inv = ROPE_THETA ** (-jnp.arange(0, rope, 2, dtype=F32) / rope)
    ang = pos.astype(F32)[:, None] * inv[None, :]
    cos, sin = jnp.cos(ang), jnp.sin(ang)
    zero = jnp.zeros((pos.shape[0], LANE - rope), F32)
    assert 2 * rope == LANE and half * 2 == rope
    return jnp.concatenate([cos, cos, zero], axis=1), jnp.concatenate([-sin, sin, zero], axis=1)


def _trunk(x3, pos0, mem, mem_k_c, mem_v_c, lat_c, kpe_c, conv_c, bk_c, bv_c, W, P, dims, bias_full):
    prompt = mem is not None
    nb, t, d = x3.shape
    m = nb * t
    depth = W["norm_mix_pre"].shape[0]
    ch, ql, kvl, rope = dims["ch"], dims["ql"], dims["kvl"], dims["rope"]
    mh, nope, dv = dims["mla_heads"], dims["nope"], dims["dv"]
    cheads, cdh = dims["c_heads"], dims["c_dh"]
    memh, memdh = dims["mem_heads"], dims["mem_dh"]
    width = W["conv_w"].shape[1]
    keep = width - 1
    halo = 32
    assert keep <= halo and t % halo == 0 and (2 * ch) % ql == 0
    band_keep = BAND_CHUNKS * CHUNK

    x = x3.reshape(m, d)
    cc, ss = _rope_tabs(pos0 + jnp.arange(t, dtype=jnp.int32), rope)
    q_scale = (nope + rope) ** -0.5
    lats, kpes, convs, bks, bvs, mks, mvs = [], [], [], [], [], [], []

    def tail_rows(h, rows):
        return h.reshape(nb, t, d)[:, t - rows:].reshape(nb * rows, d)

    h = rmsnorm(x, W["norm_mix_pre"][0])
    for l in range(depth):
        if l % 2 == 0:
            e = l // 2
            z_main = matmul(h, P["w_in_main"][e], BF16)
            z_kv = matmul(h, P["w_in_kv"][e], F32)
            if prompt:
                hist = jnp.zeros((nb, halo, ch), F32)
            else:
                hist = jnp.pad(conv_c[e], ((0, 0), (halo - keep, 0), (0, 0)))
            ya = conv_module(z_main, hist, W["conv_w"][e], W["conv_b"][e], W["conv_ln_g"][e],
                             W["conv_ln_b"][e], nb=nb, t=t, ch=ch)
            z_tail = matmul(tail_rows(h, halo), P["w_in_main"][e], F32, n=2 * ch)
            convs.append(glu_tail(z_tail, nb=nb, rows=halo, ch=ch, keep=keep))
            lat, kpe, a_kv = mla_kv_prep(z_kv, W["kv_norm_g"][e], cc, ss, t=t)
            lats.append(lat.reshape(nb, t, kvl))
            kpes.append(kpe.reshape(nb, t, rope))
            q = mla_q_proj(z_main, (2 * ch) // ql, W["q_norm_g"][e], P["w_uq"][e], cc, ss, t=t,
                           heads=mh, scale=q_scale)
            if prompt:
                lk, kv_len, mode = t, t, "block_causal"
                a_all = a_kv
            else:
                past = lat_c.shape[2]
                kv_len = past + t
                assert (kv_len - 1) // CHUNK <= (past // CHUNK), "decode keys must all be visible"
                lk = -(-kv_len // (3 * LANE)) * (3 * LANE)
                a_hist = jnp.concatenate(
                    [lat_c[e], kpe_c[e], jnp.zeros((nb, past, LANE - rope), F32)], axis=-1).astype(BF16)
                a_all = jnp.concatenate(
                    [a_hist, a_kv.reshape(nb, t, kvl + LANE),
                     jnp.zeros((nb, lk - kv_len, kvl + LANE), BF16)], axis=1).reshape(nb * lk, kvl + LANE)
                mode = "all_visible"
            kv = matmul(a_all, P["w_ukv"][e], BF16)
            yb = mla_attention(q, kv, nb=nb, tq=t, lk=lk, kv_len=kv_len, heads=mh, dv=dv, mode=mode)
            y = matmul_cat2(ya, yb, P["w_out_ab"][e], F32)
        else:
            o = l // 2
            dc = cheads * cdh
            qkv = matmul(h, P["w_qkv_c"][o], BF16)
            if prompt:
                yc = band_attention_prompt(qkv, bias_full[o], nb=nb, t=t, heads=cheads, dh=cdh)
                rows = min(band_keep, t)
            else:
                k_new = qkv[:, dc:2 * dc].reshape(nb, t, dc)
                v_new = qkv[:, 2 * dc:].reshape(nb, t, dc)
                lh = bk_c.shape[2]
                assert lh == band_keep and t == CHUNK and pos0 % CHUNK == 0
                lkp = -(-(lh + t) // LANE) * LANE
                zpad = jnp.zeros((nb, lkp - lh - t, dc), BF16)
                k_all = jnp.concatenate([bk_c[o].reshape(nb, lh, dc).astype(BF16), k_new, zpad], axis=1)
                v_all = jnp.concatenate([bv_c[o].reshape(nb, lh, dc).astype(BF16), v_new, zpad], axis=1)
                bias_s = jnp.pad(bias_full[o][:, :t, :lh + t], ((0, 0), (0, 0), (0, lkp - lh - t)),
                                 constant_values=NEG)
                yc = band_attention_step(qkv, k_all.reshape(nb * lkp, dc), v_all.reshape(nb * lkp, dc),
                                         bias_s, nb=nb, t=t, heads=cheads, dh=cdh)
                rows = t
            kv_tail = matmul(tail_rows(h, rows), P["w_qkv_c"][o], F32, n_off=dc, n=2 * dc)
            bks.append(kv_tail[:, :dc].reshape(nb, rows, cheads, cdh))
            bvs.append(kv_tail[:, dc:].reshape(nb, rows, cheads, cdh))
            y = matmul(yc, P["w_out_c"][o], F32)
        x, h = resnorm(x, y, W["norm_mix_post"][l], W["norm_mem_pre"][l])

        md = memh * memdh
        if prompt:
            ml = mem.shape[1]
            hm = rmsnorm(mem.reshape(nb * ml, d), W["norm_memkv"][l])
            mkv = matmul(hm, P["w_kv_mem"][l], F32)
            mk32, mv32 = mkv[:, :md], mkv[:, md:]
            mks.append(mk32.reshape(nb, ml, memh, memdh))
            mvs.append(mv32.reshape(nb, ml, memh, memdh))
        else:
            ml = mem_k_c.shape[2]
            mk32 = mem_k_c[l].reshape(nb * ml, md)
            mv32 = mem_v_c[l].reshape(nb * ml, md)
        qm = matmul(h, P["w_q_mem"][l], BF16)
        om = mem_attention(qm, mk32.astype(BF16), mv32.astype(BF16), nb=nb, t=t, heads=memh, dh=memdh)
        y = matmul(om, P["w_o_mem"][l], F32)
        x, h = resnorm(x, y, W["norm_mem_post"][l], W["norm_ffn_pre"][l])

        act = swiglu_up(h, P["w_gate_up"][l])
        y = matmul_ktiled(act, P["w_down"][l], F32, bk=P["ffp"] // 4)
        g_next = W["norm_mix_pre"][l + 1] if l + 1 < depth else None
        x, h = resnorm(x, y, W["norm_ffn_post"][l], g_next)

    st = lambda xs: jnp.stack(xs, 0)
    return (x.reshape(nb, t, d), st(lats), st(kpes), st(convs), st(bks), st(bvs),
            st(mks) if prompt else None, st(mvs) if prompt else None)


def kernel(x_prompt, x_sample, cache_mla_latent, cache_mla_kpe, state_conv, cache_band_k,
           cache_band_v, cache_mem_k, cache_mem_v, mem_prompt, norm_mix_pre, norm_mix_post,
           norm_mem_pre, norm_mem_post, norm_memkv, norm_ffn_pre, norm_ffn_post, w_in_ab,
           conv_w, conv_b, conv_ln_g, conv_ln_b, q_norm_g, w_uq, kv_norm_g, w_ukv, w_out_ab,
           w_qkv_c, rel_bias_c, w_out_c, w_q_mem, w_kv_mem, w_o_mem, w_gate_up, w_down):
    W = dict(norm_mix_pre=norm_mix_pre, norm_mix_post=norm_mix_post, norm_mem_pre=norm_mem_pre,
             norm_mem_post=norm_mem_post, norm_memkv=norm_memkv, norm_ffn_pre=norm_ffn_pre,
             norm_ffn_post=norm_ffn_post, w_in_ab=w_in_ab, conv_w=conv_w, conv_b=conv_b,
             conv_ln_g=conv_ln_g, conv_ln_b=conv_ln_b, q_norm_g=q_norm_g, w_uq=w_uq,
             kv_norm_g=kv_norm_g, w_ukv=w_ukv, w_out_ab=w_out_ab, w_qkv_c=w_qkv_c,
             rel_bias_c=rel_bias_c, w_out_c=w_out_c, w_q_mem=w_q_mem, w_kv_mem=w_kv_mem,
             w_o_mem=w_o_mem, w_gate_up=w_gate_up, w_down=w_down)
    ch = conv_w.shape[2]
    ql = q_norm_g.shape[1]
    kvl = kv_norm_g.shape[1]
    rope = cache_mla_kpe.shape[3]
    c_heads, c_dh = cache_band_k.shape[3], cache_band_k.shape[4]
    mem_heads, mem_dh = cache_mem_k.shape[3], cache_mem_k.shape[4]
    a_cols, b_cols, v_cols = w_uq.shape[2], w_ukv.shape[2], w_out_ab.shape[1] - ch
    mla_heads = (a_cols - b_cols + v_cols) // rope
    dims = dict(ch=ch, ql=ql, kvl=kvl, rope=rope, mla_heads=mla_heads,
                nope=(b_cols - v_cols) // mla_heads, dv=v_cols // mla_heads,
                c_heads=c_heads, c_dh=c_dh, mem_heads=mem_heads, mem_dh=mem_dh)
    assert dims["nope"] == LANE and dims["dv"] == LANE and rope == LANE // 2

    P = _prep_weights(W, dims)
    bias_full = jnp.stack([jnp.transpose(band_bias(rel_bias_c[o]), (1, 0, 2))
                           for o in range(rel_bias_c.shape[0])], 0)

    past = cache_mla_latent.shape[2]
    y_p, lat_p, kpe_p, conv_p, bk_p, bv_p, mk_p, mv_p = _trunk(
        x_prompt, 0, mem_prompt, None, None, None, None, None, None, None, W, P, dims, bias_full)
    y_s, lat_s, kpe_s, conv_s, bk_s, bv_s, _, _ = _trunk(
        x_sample, past, None, cache_mem_k, cache_mem_v, cache_mla_latent, cache_mla_kpe,
        state_conv, cache_band_k, cache_band_v, W, P, dims, bias_full)
    return (y_p, y_s, lat_p, kpe_p, conv_p, bk_p, bv_p, mk_p, mv_p, lat_s, kpe_s, conv_s, bk_s, bv_s)
```

```python
import functools

import jax
import jax.numpy as jnp
from jax import lax
from jax.experimental import pallas as pl
from jax.experimental.pallas import tpu as pltpu

F32 = jnp.float32
BF16 = jnp.bfloat16

EPS = 1e-6
NEG = -1e30
CHUNK = 64
BAND_CHUNKS = 8
ROPE_THETA = 10000.0
LOG2E = 1.4426950408889634

LANE = 128
MXU_DIM = 256
VMEM_LIMIT_BYTES = 56 * 1024 * 1024
FF_ALIGN = 1024

BAND_QB = 4 * CHUNK
BAND_KB = BAND_CHUNKS * CHUNK + BAND_QB


def _cparams(*sem):
    return pltpu.CompilerParams(dimension_semantics=sem, vmem_limit_bytes=VMEM_LIMIT_BYTES)


def _blk(n, pref):
    if n <= pref:
        return n
    b = pref
    while n % b:
        b //= 2
    assert b >= 8, (n, pref)
    return b


def _rms_f32(x, g):
    return x * lax.rsqrt(jnp.mean(x * x, axis=-1, keepdims=True) + EPS) * g


def _rmsnorm_kernel(x_ref, g_ref, o_ref):
    o_ref[...] = _rms_f32(x_ref[...].astype(F32), g_ref[...]).astype(o_ref.dtype)


def rmsnorm(x, g, out_dtype=BF16):
    m, d = x.shape
    bm = _blk(m, 256)
    return pl.pallas_call(
        _rmsnorm_kernel,
        grid=(m // bm,),
        in_specs=[pl.BlockSpec((bm, d), lambda i: (i, 0)), pl.BlockSpec((1, d), lambda i: (0, 0))],
        out_specs=pl.BlockSpec((bm, d), lambda i: (i, 0)),
        out_shape=jax.ShapeDtypeStruct((m, d), out_dtype),
        compiler_params=_cparams("parallel"),
        name="rmsnorm",
    )(x, g.reshape(1, d))


def _resnorm_kernel(x_ref, y_ref, gp_ref, gn_ref, xo_ref, h_ref):
    xn = x_ref[...] + _rms_f32(y_ref[...], gp_ref[...])
    xo_ref[...] = xn
    h_ref[...] = _rms_f32(xn, gn_ref[...]).astype(h_ref.dtype)


def _resnorm_last_kernel(x_ref, y_ref, gp_ref, xo_ref):
    xo_ref[...] = x_ref[...] + _rms_f32(y_ref[...], gp_ref[...])


def resnorm(x, y, g_post, g_next):
    m, d = x.shape
    bm = _blk(m, 256)
    row = pl.BlockSpec((bm, d), lambda i: (i, 0))
    vec = pl.BlockSpec((1, d), lambda i: (0, 0))
    if g_next is None:
        return pl.pallas_call(
            _resnorm_last_kernel, grid=(m // bm,), in_specs=[row, row, vec], out_specs=row,
            out_shape=jax.ShapeDtypeStruct((m, d), F32), compiler_params=_cparams("parallel"),
            name="resnorm_last",
        )(x, y, g_post.reshape(1, d)), None
    return pl.pallas_call(
        _resnorm_kernel, grid=(m // bm,), in_specs=[row, row, vec, vec], out_specs=[row, row],
        out_shape=[jax.ShapeDtypeStruct((m, d), F32), jax.ShapeDtypeStruct((m, d), BF16)],
        compiler_params=_cparams("parallel"), name="resnorm",
    )(x, y, g_post.reshape(1, d), g_next.reshape(1, d))


def _mm_kernel(a_ref, w_ref, o_ref, *, scale, scale_blocks):
    acc = jnp.dot(a_ref[...], w_ref[...], preferred_element_type=F32)
    if scale is not None:
        sc = scale if scale_blocks is None else jnp.where(pl.program_id(1) < scale_blocks, scale, 1.0)
        acc = acc * sc
    o_ref[...] = acc.astype(o_ref.dtype)


def matmul(a, w, out_dtype, *, n_off=0, n=None, scale=None, scale_cols=None, bm_pref=1024, bn_pref=1024):
    m, k = a.shape
    n = w.shape[1] - n_off if n is None else n
    bm = _blk(m, bm_pref)
    bn = _blk(n, bn_pref)
    while n_off % bn:
        bn //= 2
    assert bn % LANE == 0 or bn == w.shape[1]
    joff = n_off // bn
    scale_blocks = None
    if scale_cols is not None:
        assert scale_cols % bn == 0
        scale_blocks = scale_cols // bn
    return pl.pallas_call(
        functools.partial(_mm_kernel, scale=scale, scale_blocks=scale_blocks),
        grid=(m // bm, n // bn),
        in_specs=[pl.BlockSpec((bm, k), lambda i, j: (i, 0)),
                  pl.BlockSpec((k, bn), lambda i, j: (0, j + joff))],
        out_specs=pl.BlockSpec((bm, bn), lambda i, j: (i, j)),
        out_shape=jax.ShapeDtypeStruct((m, n), out_dtype),
        compiler_params=_cparams("parallel", "parallel"),
        name="matmul",
    )(a, w)


def _mm2_kernel(a1_ref, a2_ref, w1_ref, w2_ref, o_ref):
    acc = jnp.dot(a1_ref[...], w1_ref[...], preferred_element_type=F32)
    acc = acc + jnp.dot(a2_ref[...], w2_ref[...], preferred_element_type=F32)
    o_ref[...] = acc.astype(o_ref.dtype)


def matmul_cat2(a1, a2, w, out_dtype):
    m, k1 = a1.shape
    assert a2.shape == (m, k1) and w.shape[0] == 2 * k1
    n = w.shape[1]
    bm = _blk(m, 1024)
    bn = _blk(n, 1024)
    return pl.pallas_call(
        _mm2_kernel,
        grid=(m // bm, n // bn),
        in_specs=[pl.BlockSpec((bm, k1), lambda i, j: (i, 0)),
                  pl.BlockSpec((bm, k1), lambda i, j: (i, 0)),
                  pl.BlockSpec((k1, bn), lambda i, j: (0, j)),
                  pl.BlockSpec((k1, bn), lambda i, j: (1, j))],
        out_specs=pl.BlockSpec((bm, bn), lambda i, j: (i, j)),
        out_shape=jax.ShapeDtypeStruct((m, n), out_dtype),
        compiler_params=_cparams("parallel", "parallel"),
        name="matmul_cat2",
    )(a1, a2, w, w)


def _mm_acc_kernel(a_ref, w_ref, o_ref, acc_ref, *, nk):
    kk = pl.program_id(2)
    part = jnp.dot(a_ref[...], w_ref[...], preferred_element_type=F32)

    @pl.when(kk == 0)
    def _():
        acc_ref[...] = part

    @pl.when(kk > 0)
    def _():
        acc_ref[...] += part

    @pl.when(kk == nk - 1)
    def _():
        o_ref[...] = acc_ref[...].astype(o_ref.dtype)


def matmul_ktiled(a, w, out_dtype, *, bk):
    m, k = a.shape
    n = w.shape[1]
    bm = _blk(m, 1024)
    bn = _blk(n, 1024)
    assert k % bk == 0
    nk = k // bk
    return pl.pallas_call(
        functools.partial(_mm_acc_kernel, nk=nk),
        grid=(m // bm, n // bn, nk),
        in_specs=[pl.BlockSpec((bm, bk), lambda i, j, kk: (i, kk)),
                  pl.BlockSpec((bk, bn), lambda i, j, kk: (kk, j))],
        out_specs=pl.BlockSpec((bm, bn), lambda i, j, kk: (i, j)),
        out_shape=jax.ShapeDtypeStruct((m, n), out_dtype),
        scratch_shapes=[pltpu.VMEM((bm, bn), F32)],
        compiler_params=_cparams("parallel", "parallel", "arbitrary"),
        name="matmul_ktiled",
    )(a, w)


def _swiglu_kernel(a_ref, wg_ref, wu_ref, o_ref):
    a = a_ref[...]
    g = jnp.dot(a, wg_ref[...], preferred_element_type=F32)
    u = jnp.dot(a, wu_ref[...], preferred_element_type=F32)
    o_ref[...] = (g * jax.nn.sigmoid(g) * u).astype(o_ref.dtype)


def swiglu_up(h, w_gu):
    m, k = h.shape
    ff = w_gu.shape[1] // 2
    bm = _blk(m, 1024)
    bn = _blk(ff, 512)
    nj = ff // bn
    return pl.pallas_call(
        _swiglu_kernel,
        grid=(m // bm, nj),
        in_specs=[pl.BlockSpec((bm, k), lambda i, j: (i, 0)),
                  pl.BlockSpec((k, bn), lambda i, j: (0, j)),
                  pl.BlockSpec((k, bn), lambda i, j: (0, j + nj))],
        out_specs=pl.BlockSpec((bm, bn), lambda i, j: (i, j)),
        out_shape=jax.ShapeDtypeStruct((m, ff), BF16),
        compiler_params=_cparams("parallel", "parallel"),
        name="swiglu_up",
    )(h, w_gu, w_gu)


def _conv_kernel(a_ref, b_ref, ah_ref, bh_ref, hist_ref, cw_ref, cb_ref, lg_ref, lb_ref, o_ref,
                 buf_ref, y_ref, *, tb, width, halo, tt):
    t_blk = pl.program_id(1)
    a = a_ref[...].astype(F32)
    u_cur = a * jax.nn.sigmoid(b_ref[...].astype(F32))
    @pl.when(t_blk == 0)
    def _():
        buf_ref[0:halo, :] = hist_ref[...]

    @pl.when(t_blk > 0)
    def _():
        ah = ah_ref[...].astype(F32)
        buf_ref[0:halo, :] = ah * jax.nn.sigmoid(bh_ref[...].astype(F32))

    buf_ref[halo:halo + tb, :] = u_cur
    ch = a.shape[1]
    base = halo - (width - 1)

    def lane_tile(ct, carry):
        c0 = pl.multiple_of(ct * LANE, LANE)
        bias = cb_ref[:, pl.ds(c0, LANE)]
        for t0 in range(0, tb, tt):
            acc = jnp.broadcast_to(bias, (tt, LANE))
            for w in range(width):
                acc = acc + cw_ref[w:w + 1, pl.ds(c0, LANE)] * buf_ref[pl.ds(base + t0 + w, tt), pl.ds(c0, LANE)]
            y_ref[pl.ds(t0, tt), pl.ds(c0, LANE)] = acc
        return carry

    lax.fori_loop(0, ch // LANE, lane_tile, 0)
    y = y_ref[...]
    mu = jnp.mean(y, axis=-1, keepdims=True)
    yc = y - mu
    var = jnp.mean(yc * yc, axis=-1, keepdims=True)
    z = yc * lax.rsqrt(var + EPS) * lg_ref[...] + lb_ref[...]
    o_ref[...] = (z * jax.nn.sigmoid(z)).astype(o_ref.dtype)


def conv_module(z_main, hist, conv_w, conv_b, ln_g, ln_b, *, nb, t, ch):
    width = conv_w.shape[0]
    halo = hist.shape[1]
    tb = _blk(t, 256)
    tt = _blk(tb, 64)
    nt = t // tb
    hb = tb // halo
    cw = jnp.pad(conv_w, ((0, halo - width), (0, 0)))
    row = lambda c: pl.BlockSpec((tb, ch), lambda b, i: (b * nt + i, c))
    prev = lambda c: pl.BlockSpec((halo, ch), lambda b, i: (jnp.maximum((b * nt + i) * hb - 1, 0), c))
    vec = pl.BlockSpec((1, ch), lambda b, i: (0, 0))
    return pl.pallas_call(
        functools.partial(_conv_kernel, tb=tb, width=width, halo=halo, tt=tt),
        grid=(nb, nt),
        in_specs=[row(0), row(1), prev(0), prev(1),
                  pl.BlockSpec((None, halo, ch), lambda b, i: (b, 0, 0)),
                  pl.BlockSpec((halo, ch), lambda b, i: (0, 0)), vec, vec, vec],
        out_specs=pl.BlockSpec((tb, ch), lambda b, i: (b * nt + i, 0)),
        out_shape=jax.ShapeDtypeStruct((nb * t, ch), BF16),
        scratch_shapes=[pltpu.VMEM((halo + tb, ch), F32), pltpu.VMEM((tb, ch), F32)],
        compiler_params=_cparams("parallel", "parallel"),
        name="conv_module",
    )(z_main, z_main, z_main, z_main, hist, cw, conv_b.reshape(1, ch), ln_g.reshape(1, ch),
      ln_b.reshape(1, ch))


def _glu_tail_kernel(z_ref, o_ref, *, ch, keep):
    z = z_ref[...]
    u = z[:, :ch] * jax.nn.sigmoid(z[:, ch:])
    o_ref[...] = u[u.shape[0] - keep:, :]


def glu_tail(z_tail, *, nb, rows, ch, keep):
    return pl.pallas_call(
        functools.partial(_glu_tail_kernel, ch=ch, keep=keep),
        grid=(nb,),
        in_specs=[pl.BlockSpec((rows, 2 * ch), lambda b: (b, 0))],
        out_specs=pl.BlockSpec((None, keep, ch), lambda b: (b, 0, 0)),
        out_shape=jax.ShapeDtypeStruct((nb, keep, ch), F32),
        compiler_params=_cparams("parallel"),
        name="glu_tail",
    )(z_tail)


def _rope_pairs(v, cc, ss):
    return v * cc + pltpu.roll(v, LANE // 2, axis=1) * ss


def _mla_kv_prep_kernel(z_ref, g_ref, cc_ref, ss_ref, lat_ref, kpe_ref, a_ref, *, kvl):
    z = z_ref[...]
    lat = _rms_f32(z[:, :kvl], g_ref[...])
    rot = _rope_pairs(z[:, kvl:kvl + LANE], cc_ref[...], ss_ref[...])
    lat_ref[...] = lat
    kpe_ref[...] = rot[:, :LANE // 2]
    a_ref[:, :kvl] = lat.astype(a_ref.dtype)
    a_ref[:, kvl:] = rot.astype(a_ref.dtype)


def mla_kv_prep(z_kv, g, cc, ss, *, t):
    m, wdt = z_kv.shape
    kvl = wdt - LANE
    bm = _blk(t, 256)
    nt = t // bm
    row = lambda wd: pl.BlockSpec((bm, wd), lambda i: (i, 0))
    tab = pl.BlockSpec((bm, LANE), lambda i: (i % nt, 0))
    return pl.pallas_call(
        functools.partial(_mla_kv_prep_kernel, kvl=kvl),
        grid=(m // bm,),
        in_specs=[row(wdt), pl.BlockSpec((1, kvl), lambda i: (0, 0)), tab, tab],
        out_specs=[row(kvl), row(LANE // 2), row(wdt)],
        out_shape=[jax.ShapeDtypeStruct((m, kvl), F32), jax.ShapeDtypeStruct((m, LANE // 2), F32),
                   jax.ShapeDtypeStruct((m, wdt), BF16)],
        compiler_params=_cparams("parallel"),
        name="mla_kv_prep",
    )(z_kv, g.reshape(1, kvl), cc, ss)


def _mla_q_kernel(z_ref, g_ref, w_ref, cc_ref, ss_ref, o_ref, *, heads, scale):
    a = _rms_f32(z_ref[...].astype(F32), g_ref[...]).astype(BF16)
    acc = jnp.dot(a, w_ref[...], preferred_element_type=F32)
    cc = cc_ref[...]
    ss = ss_ref[...]
    for h in range(heads):
        c0 = h * MXU_DIM
        o_ref[:, c0:c0 + LANE] = (acc[:, c0:c0 + LANE] * scale).astype(o_ref.dtype)
        rot = _rope_pairs(acc[:, c0 + LANE:c0 + MXU_DIM], cc, ss)
        o_ref[:, c0 + LANE:c0 + MXU_DIM] = (rot * scale).astype(o_ref.dtype)


def mla_q_proj(z_main, col_blk, g, w_uq_ext, cc, ss, *, t, heads, scale):
    m = z_main.shape[0]
    ql, n = w_uq_ext.shape
    bm = _blk(t, 512)
    nt = t // bm
    tab = pl.BlockSpec((bm, LANE), lambda i: (i % nt, 0))
    return pl.pallas_call(
        functools.partial(_mla_q_kernel, heads=heads, scale=scale),
        grid=(m // bm,),
        in_specs=[pl.BlockSpec((bm, ql), lambda i: (i, col_blk)),
                  pl.BlockSpec((1, ql), lambda i: (0, 0)),
                  pl.BlockSpec((ql, n), lambda i: (0, 0)), tab, tab],
        out_specs=pl.BlockSpec((bm, n), lambda i: (i, 0)),
        out_shape=jax.ShapeDtypeStruct((m, n), BF16),
        compiler_params=_cparams("parallel"),
        name="mla_q_proj",
    )(z_main, g.reshape(1, ql), w_uq_ext, cc, ss)


def _mla_attn_kernel(q_ref, k_ref, v_ref, o_ref, m_ref, l_ref, acc_ref, *, heads, dv, bq, bk, nk,
                     mode, kv_len):
    i = pl.program_id(1)
    j = pl.program_id(2)

    @pl.when(j == 0)
    def _():
        m_ref[...] = jnp.full(m_ref.shape, NEG, F32)
        l_ref[...] = jnp.zeros(l_ref.shape, F32)
        acc_ref[...] = jnp.zeros(acc_ref.shape, F32)

    def step(mask):
        for h in range(heads):
            qh = q_ref[:, h * MXU_DIM:(h + 1) * MXU_DIM]
            kh = k_ref[:, h * MXU_DIM:(h + 1) * MXU_DIM]
            s = lax.dot_general(qh, kh, (((1,), (1,)), ((), ())), preferred_element_type=F32)
            if mask is not None:
                s = jnp.where(mask, s, NEG)
            tiles = [s[:, t * LANE:(t + 1) * LANE] for t in range(bk // LANE)]
            mx = tiles[0]
            for tl in tiles[1:]:
                mx = jnp.maximum(mx, tl)
            m_prev = m_ref[h]
            m_new = jnp.maximum(m_prev, jnp.max(mx, axis=1, keepdims=True))
            alpha = jnp.exp2(m_prev - m_new)
            ps = [jnp.exp2(tl - m_new) for tl in tiles]
            lsum = ps[0]
            for p in ps[1:]:
                lsum = lsum + p
            l_ref[h] = alpha * l_ref[h] + lsum
            pb = jnp.concatenate(ps, axis=1).astype(BF16)
            pv = jnp.dot(pb, v_ref[:, h * dv:(h + 1) * dv], preferred_element_type=F32)
            acc_ref[:, h * dv:(h + 1) * dv] = alpha * acc_ref[:, h * dv:(h + 1) * dv] + pv
            m_ref[h] = m_new

    if mode == "block_causal":
        @pl.when(j < i)
        def _():
            step(None)

        @pl.when(j == i)
        def _():
            qc = lax.broadcasted_iota(jnp.int32, (bq, bk), 0) // CHUNK
            kc = lax.broadcasted_iota(jnp.int32, (bq, bk), 1) // CHUNK
            step(kc <= qc)
    else:
        @pl.when(j < nk - 1)
        def _():
            step(None)

        @pl.when(j == nk - 1)
        def _():
            kpos = (nk - 1) * bk + lax.broadcasted_iota(jnp.int32, (bq, bk), 1)
            step(kpos < kv_len if kv_len < nk * bk else None)

    @pl.when(j == nk - 1)
    def _():
        for h in range(heads):
            l = jnp.sum(l_ref[h], axis=1, keepdims=True)
            o_ref[:, h * dv:(h + 1) * dv] = (acc_ref[:, h * dv:(h + 1) * dv] / l).astype(o_ref.dtype)


def mla_attention(q, kv, *, nb, tq, lk, kv_len, heads, dv, mode):
    kw = heads * MXU_DIM
    vw = heads * dv
    assert kw % vw == 0 and dv == LANE
    bq = _blk(tq, 512)
    if mode == "block_causal":
        assert lk == tq and kv_len == lk and bq % CHUNK == 0
        bk = bq
    else:
        bk = _blk(lk, 768)
    nq, nk = tq // bq, lk // bk
    assert kv_len > (nk - 1) * bk, "key padding must sit in the last key block"
    if mode == "block_causal":
        kidx = lambda b, i, j: (b * nk + jnp.minimum(j, i), 0)
        vidx = lambda b, i, j: (b * nk + jnp.minimum(j, i), kw // vw)
    else:
        kidx = lambda b, i, j: (b * nk + j, 0)
        vidx = lambda b, i, j: (b * nk + j, kw // vw)
    return pl.pallas_call(
        functools.partial(_mla_attn_kernel, heads=heads, dv=dv, bq=bq, bk=bk, nk=nk, mode=mode,
                          kv_len=kv_len),
        grid=(nb, nq, nk),
        in_specs=[pl.BlockSpec((bq, kw), lambda b, i, j: (b * nq + i, 0)),
                  pl.BlockSpec((bk, kw), kidx),
                  pl.BlockSpec((bk, vw), vidx)],
        out_specs=pl.BlockSpec((bq, vw), lambda b, i, j: (b * nq + i, 0)),
        out_shape=jax.ShapeDtypeStruct((nb * tq, vw), BF16),
        scratch_shapes=[pltpu.VMEM((heads, bq, LANE), F32), pltpu.VMEM((heads, bq, LANE), F32),
                        pltpu.VMEM((bq, vw), F32)],
        compiler_params=_cparams("parallel", "parallel", "arbitrary"),
        name="mla_attention",
    )(q, kv, kv)


def _band_bias_kernel(t_ref, o_ref, *, rows, max_rel, kb, past):
    t = t_ref[...]
    hi = t.astype(BF16)
    r1 = t - hi.astype(F32)
    mid = r1.astype(BF16)
    lo = (r1 - mid.astype(F32)).astype(BF16)
    nrel = t.shape[1]
    j = lax.broadcasted_iota(jnp.int32, (nrel, kb), 1)
    r = lax.broadcasted_iota(jnp.int32, (nrel, kb), 0)
    kc = lax.broadcasted_iota(jnp.int32, (t.shape[0], kb), 1) // CHUNK - past // CHUNK
    for rr in range(rows):
        i = pl.program_id(0) * rows + rr
        idx = jnp.clip(i - j + past, -max_rel, max_rel) + max_rel
        onehot = jnp.where(r == idx, 1.0, 0.0).astype(BF16)
        vals = (jnp.dot(hi, onehot, preferred_element_type=F32)
                + jnp.dot(mid, onehot, preferred_element_type=F32)
                + jnp.dot(lo, onehot, preferred_element_type=F32))
        qc = i // CHUNK
        ok = (kc <= qc) & (kc >= qc - BAND_CHUNKS)
        o_ref[rr] = jnp.where(ok, vals * LOG2E, NEG)


def band_bias(table):
    heads, nrel = table.shape
    max_rel = (nrel - 1) // 2
    nrel_p = -(-nrel // LANE) * LANE
    tp = jnp.pad(table, ((0, 0), (0, nrel_p - nrel)))
    rows = 8
    past = BAND_CHUNKS * CHUNK
    return pl.pallas_call(
        functools.partial(_band_bias_kernel, rows=rows, max_rel=max_rel, kb=BAND_KB, past=past),
        grid=(BAND_QB // rows,),
        in_specs=[pl.BlockSpec((heads, nrel_p), lambda i: (0, 0))],
        out_specs=pl.BlockSpec((rows, heads, BAND_KB), lambda i: (i, 0, 0)),
        out_shape=jax.ShapeDtypeStruct((BAND_QB, heads, BAND_KB), F32),
        compiler_params=_cparams("parallel"),
        name="band_bias",
    )(tp)


def _softmax_pv(tiles, v_parts):
    mx = tiles[0]
    for tl in tiles[1:]:
        mx = jnp.maximum(mx, tl)
    m = jnp.max(mx, axis=1, keepdims=True)
    ps = [jnp.exp2(tl - m) for tl in tiles]
    lsum = ps[0]
    for p in ps[1:]:
        lsum = lsum + p
    l = jnp.sum(lsum, axis=1, keepdims=True)
    per = len(tiles) // len(v_parts)
    o = None
    for n, vp in enumerate(v_parts):
        pb = jnp.concatenate(ps[n * per:(n + 1) * per], axis=1).astype(BF16)
        part = jnp.dot(pb, vp, preferred_element_type=F32)
        o = part if o is None else o + part
    return o / l


def _lane_tiles(s):
    return [s[:, c:c + LANE] for c in range(0, s.shape[1], LANE)]


def _band_prompt_kernel(q_ref, k0_ref, k1_ref, k2_ref, v0_ref, v1_ref, v2_ref, bias_ref, o_ref, *,
                        group, dh):
    qb = pl.program_id(2)
    pen = [jnp.where(qb < 2, NEG, 0.0), jnp.where(qb < 1, NEG, 0.0), None]
    for g in range(group):
        sl = slice(g * dh, (g + 1) * dh)
        qh = q_ref[:, sl]
        tiles = []
        for n, kr in enumerate((k0_ref, k1_ref, k2_ref)):
            part = lax.dot_general(qh, kr[:, sl], (((1,), (1,)), ((), ())), preferred_element_type=F32)
            for c in range(0, BAND_QB, LANE):
                tl = part[:, c:c + LANE] + bias_ref[g, :, n * BAND_QB + c:n * BAND_QB + c + LANE]
                tiles.append(tl if pen[n] is None else tl + pen[n])
        o = _softmax_pv(tiles, [v0_ref[:, sl], v1_ref[:, sl], v2_ref[:, sl]])
        o_ref[:, sl] = o.astype(o_ref.dtype)


def band_attention_prompt(qkv, bias, *, nb, t, heads, dh):
    d = heads * dh
    group = min(heads, 8)
    ng = heads // group
    gw = group * dh
    nq = t // BAND_QB
    assert t % BAND_QB == 0 and BAND_KB == 3 * BAND_QB

    def kspec(back, col0):
        return pl.BlockSpec((BAND_QB, gw),
                            lambda hg, b, i: (b * nq + jnp.maximum(i - back, 0), col0 + hg))

    return pl.pallas_call(
        functools.partial(_band_prompt_kernel, group=group, dh=dh),
        grid=(ng, nb, nq),
        in_specs=[pl.BlockSpec((BAND_QB, gw), lambda hg, b, i: (b * nq + i, hg)),
                  kspec(2, ng), kspec(1, ng), kspec(0, ng),
                  kspec(2, 2 * ng), kspec(1, 2 * ng), kspec(0, 2 * ng),
                  pl.BlockSpec((group, BAND_QB, BAND_KB), lambda hg, b, i: (hg, 0, 0))],
        out_specs=pl.BlockSpec((BAND_QB, gw), lambda hg, b, i: (b * nq + i, hg)),
        out_shape=jax.ShapeDtypeStruct((nb * t, d), BF16),
        compiler_params=_cparams("parallel", "parallel", "parallel"),
        name="band_attention_prompt",
    )(qkv, qkv, qkv, qkv, qkv, qkv, qkv, bias)


def _band_step_kernel(q_ref, k_ref, v_ref, bias_ref, o_ref, *, group, dh):
    for g in range(group):
        sl = slice(g * dh, (g + 1) * dh)
        s = lax.dot_general(q_ref[:, sl], k_ref[:, sl], (((1,), (1,)), ((), ())),
                            preferred_element_type=F32) + bias_ref[g]
        o = _softmax_pv(_lane_tiles(s), [v_ref[:, sl]])
        o_ref[:, sl] = o.astype(o_ref.dtype)


def band_attention_step(qkv, k_all, v_all, bias, *, nb, t, heads, dh):
    d = heads * dh
    group = min(heads, 8)
    ng = heads // group
    gw = group * dh
    lk = k_all.shape[0] // nb
    return pl.pallas_call(
        functools.partial(_band_step_kernel, group=group, dh=dh),
        grid=(ng, nb),
        in_specs=[pl.BlockSpec((t, gw), lambda hg, b: (b, hg)),
                  pl.BlockSpec((lk, gw), lambda hg, b: (b, hg)),
                  pl.BlockSpec((lk, gw), lambda hg, b: (b, hg)),
                  pl.BlockSpec((group, t, lk), lambda hg, b: (hg, 0, 0))],
        out_specs=pl.BlockSpec((t, gw), lambda hg, b: (b, hg)),
        out_shape=jax.ShapeDtypeStruct((nb * t, d), BF16),
        compiler_params=_cparams("parallel", "parallel"),
        name="band_attention_step",
    )(qkv, k_all, v_all, bias)


def _mem_attn_kernel(q_ref, k_ref, v_ref, o_ref, *, heads, dh):
    for h in range(heads):
        sl = slice(h * dh, (h + 1) * dh)
        s = lax.dot_general(q_ref[:, sl], k_ref[:, sl], (((1,), (1,)), ((), ())),
                            preferred_element_type=F32)
        o = _softmax_pv(_lane_tiles(s), [v_ref[:, sl]])
        o_ref[:, sl] = o.astype(o_ref.dtype)


def mem_attention(q, mk, mv, *, nb, t, heads, dh):
    d = heads * dh
    ml = mk.shape[0] // nb
    bq = _blk(t, 512)
    nq = t // bq
    return pl.pallas_call(
        functools.partial(_mem_attn_kernel, heads=heads, dh=dh),
        grid=(nb, nq),
        in_specs=[pl.BlockSpec((bq, d), lambda b, i: (b * nq + i, 0)),
                  pl.BlockSpec((ml, d), lambda b, i: (b, 0)),
                  pl.BlockSpec((ml, d), lambda b, i: (b, 0))],
        out_specs=pl.BlockSpec((bq, d), lambda b, i: (b * nq + i, 0)),
        out_shape=jax.ShapeDtypeStruct((nb * t, d), BF16),
        compiler_params=_cparams("parallel", "parallel"),
        name="mem_attention",
    )(q, mk, mv)


def _prep_weights(W, dims):
    heads, nope, rope, dv, kvl = dims["mla_heads"], dims["nope"], dims["rope"], dims["dv"], dims["kvl"]
    ch, ql = dims["ch"], dims["ql"]
    half = rope // 2
    P = {}
    w_in = W["w_in_ab"]
    s2 = 2 * ch + ql
    P["w_in_main"] = w_in[:, :, :s2].astype(BF16)
    kpe0 = s2 + kvl
    P["w_in_kv"] = jnp.concatenate(
        [w_in[:, :, s2:], w_in[:, :, kpe0 + half:kpe0 + rope], w_in[:, :, kpe0:kpe0 + half]],
        axis=-1).astype(BF16)
    ne = w_in.shape[0]
    wq = W["w_uq"].reshape(ne, ql, heads, nope + rope)
    pe = wq[..., nope:]
    pe_sw = jnp.concatenate([pe[..., half:], pe[..., :half]], axis=-1)
    P["w_uq"] = jnp.concatenate([wq[..., :nope], pe, pe_sw], axis=-1).reshape(
        ne, ql, heads * MXU_DIM).astype(BF16)
    wkv = W["w_ukv"].reshape(ne, kvl, heads, nope + dv)
    kslot = jnp.pad(wkv[..., :nope], ((0, 0), (0, LANE), (0, 0), (0, MXU_DIM - nope)))
    eye = jnp.zeros((LANE, MXU_DIM), F32).at[jnp.arange(rope), nope + jnp.arange(rope)].set(1.0)
    kslot = kslot + jnp.pad(eye, ((kvl, 0), (0, 0)))[None, :, None, :]
    vpart = jnp.pad(wkv[..., nope:], ((0, 0), (0, LANE), (0, 0), (0, 0)))
    P["w_ukv"] = jnp.concatenate(
        [kslot.reshape(ne, kvl + LANE, heads * MXU_DIM), vpart.reshape(ne, kvl + LANE, heads * dv)],
        axis=-1).astype(BF16)
    for name in ("w_out_ab", "w_qkv_c", "w_out_c", "w_q_mem", "w_kv_mem", "w_o_mem"):
        P[name] = W[name].astype(BF16)
    ff = W["w_down"].shape[1]
    ffp = -(-ff // FF_ALIGN) * FF_ALIGN
    wgu = W["w_gate_up"]
    nl, dm = wgu.shape[0], wgu.shape[1]
    P["w_gate_up"] = jnp.pad(wgu.reshape(nl, dm, 2, ff), ((0, 0), (0, 0), (0, 0), (0, ffp - ff))).astype(
        BF16).reshape(nl, dm, 2 * ffp)
    P["w_down"] = jnp.pad(W["w_down"], ((0, 0), (0, ffp - ff), (0, 0))).astype(BF16)
    P["ffp"] = ffp
    return P


def _rope_tabs(pos, rope):
    half = rope // 2
    inv = ROPE_THETA ** (-jnp.arange(0, rope, 2, dtype=F32) / rope)
    ang = pos.astype(F32)[:, None] * inv[None, :]
    cos, sin = jnp.cos(ang), jnp.sin(ang)
    zero = jnp.zeros((pos.shape[0], LANE - rope), F32)
    assert 2 * rope == LANE and half * 2 == rope
    return jnp.concatenate([cos, cos, zero], axis=1), jnp.concatenate([-sin, sin, zero], axis=1)


def _trunk(x3, pos0, mem, mem_k_c, mem_v_c, lat_c, kpe_c, conv_c, bk_c, bv_c, W, P, dims, bias_full):
    prompt = mem is not None
    nb, t, d = x3.shape
    m = nb * t
    depth = W["norm_mix_pre"].shape[0]
    ch, ql, kvl, rope = dims["ch"], dims["ql"], dims["kvl"], dims["rope"]
    mh, nope, dv = dims["mla_heads"], dims["nope"], dims["dv"]
    cheads, cdh = dims["c_heads"], dims["c_dh"]
    memh, memdh = dims["mem_heads"], dims["mem_dh"]
    width = W["conv_w"].shape[1]
    keep = width - 1
    halo = 32
    assert keep <= halo and t % halo == 0 and (2 * ch) % ql == 0
    band_keep = BAND_CHUNKS * CHUNK

    x = x3.reshape(m, d)
    cc, ss = _rope_tabs(pos0 + jnp.arange(t, dtype=jnp.int32), rope)
    q_scale = (nope + rope) ** -0.5 * LOG2E
    lats, kpes, convs, bks, bvs, mks, mvs = [], [], [], [], [], [], []

    def tail_rows(h, rows):
        return h.reshape(nb, t, d)[:, t - rows:].reshape(nb * rows, d)

    h = rmsnorm(x, W["norm_mix_pre"][0])
    for l in range(depth):
        if l % 2 == 0:
            e = l // 2
            z_main = matmul(h, P["w_in_main"][e], BF16)
            z_kv = matmul(h, P["w_in_kv"][e], F32)
            if prompt:
                hist = jnp.zeros((nb, halo, ch), F32)
            else:
                hist = jnp.pad(conv_c[e], ((0, 0), (halo - keep, 0), (0, 0)))
            ya = conv_module(z_main, hist, W["conv_w"][e], W["conv_b"][e], W["conv_ln_g"][e],
                             W["conv_ln_b"][e], nb=nb, t=t, ch=ch)
            z_tail = matmul(tail_rows(h, halo), P["w_in_main"][e], F32, n=2 * ch)
            convs.append(glu_tail(z_tail, nb=nb, rows=halo, ch=ch, keep=keep))
            lat, kpe, a_kv = mla_kv_prep(z_kv, W["kv_norm_g"][e], cc, ss, t=t)
            lats.append(lat.reshape(nb, t, kvl))
            kpes.append(kpe.reshape(nb, t, rope))
            q = mla_q_proj(z_main, (2 * ch) // ql, W["q_norm_g"][e], P["w_uq"][e], cc, ss, t=t,
                           heads=mh, scale=q_scale)
            if prompt:
                lk, kv_len, mode = t, t, "block_causal"
                a_all = a_kv
            else:
                past = lat_c.shape[2]
                kv_len = past + t
                assert (kv_len - 1) // CHUNK <= (past // CHUNK), "decode keys must all be visible"
                lk = -(-kv_len // (3 * LANE)) * (3 * LANE)
                a_hist = jnp.concatenate(
                    [lat_c[e], kpe_c[e], jnp.zeros((nb, past, LANE - rope), F32)], axis=-1).astype(BF16)
                a_all = jnp.concatenate(
                    [a_hist, a_kv.reshape(nb, t, kvl + LANE),
                     jnp.zeros((nb, lk - kv_len, kvl + LANE), BF16)], axis=1).reshape(nb * lk, kvl + LANE)
                mode = "all_visible"
            kv = matmul(a_all, P["w_ukv"][e], BF16)
            yb = mla_attention(q, kv, nb=nb, tq=t, lk=lk, kv_len=kv_len, heads=mh, dv=dv, mode=mode)
            y = matmul_cat2(ya, yb, P["w_out_ab"][e], F32)
        else:
            o = l // 2
            dc = cheads * cdh
            qkv = matmul(h, P["w_qkv_c"][o], BF16, scale=cdh ** -0.5 * LOG2E, scale_cols=dc)
            if prompt:
                yc = band_attention_prompt(qkv, bias_full[o], nb=nb, t=t, heads=cheads, dh=cdh)
                rows = min(band_keep, t)
            else:
                k_new = qkv[:, dc:2 * dc].reshape(nb, t, dc)
                v_new = qkv[:, 2 * dc:].reshape(nb, t, dc)
                lh = bk_c.shape[2]
                assert lh == band_keep and t == CHUNK and pos0 % CHUNK == 0
                lkp = -(-(lh + t) // LANE) * LANE
                zpad = jnp.zeros((nb, lkp - lh - t, dc), BF16)
                k_all = jnp.concatenate([bk_c[o].reshape(nb, lh, dc).astype(BF16), k_new, zpad], axis=1)
                v_all = jnp.concatenate([bv_c[o].reshape(nb, lh, dc).astype(BF16), v_new, zpad], axis=1)
                bias_s = jnp.pad(bias_full[o][:, :t, :lh + t], ((0, 0), (0, 0), (0, lkp - lh - t)),
                                 constant_values=NEG)
                yc = band_attention_step(qkv, k_all.reshape(nb * lkp, dc), v_all.reshape(nb * lkp, dc),
                                         bias_s, nb=nb, t=t, heads=cheads, dh=cdh)
                rows = t
            kv_tail = matmul(tail_rows(h, rows), P["w_qkv_c"][o], F32, n_off=dc, n=2 * dc)
            bks.append(kv_tail[:, :dc].reshape(nb, rows, cheads, cdh))
            bvs.append(kv_tail[:, dc:].reshape(nb, rows, cheads, cdh))
            y = matmul(yc, P["w_out_c"][o], F32)
        x, h = resnorm(x, y, W["norm_mix_post"][l], W["norm_mem_pre"][l])

        md = memh * memdh
        if prompt:
            ml = mem.shape[1]
            hm = rmsnorm(mem.reshape(nb * ml, d), W["norm_memkv"][l])
            mkv = matmul(hm, P["w_kv_mem"][l], F32)
            mk32, mv32 = mkv[:, :md], mkv[:, md:]
            mks.append(mk32.reshape(nb, ml, memh, memdh))
            mvs.append(mv32.reshape(nb, ml, memh, memdh))
        else:
            ml = mem_k_c.shape[2]
            mk32 = mem_k_c[l].reshape(nb * ml, md)
            mv32 = mem_v_c[l].reshape(nb * ml, md)
        qm = matmul(h, P["w_q_mem"][l], BF16, scale=memdh ** -0.5 * LOG2E)
        om = mem_attention(qm, mk32.astype(BF16), mv32.astype(BF16), nb=nb, t=t, heads=memh, dh=memdh)
        y = matmul(om, P["w_o_mem"][l], F32)
        x, h = resnorm(x, y, W["norm_mem_post"][l], W["norm_ffn_pre"][l])

        act = swiglu_up(h, P["w_gate_up"][l])
        y = matmul_ktiled(act, P["w_down"][l], F32, bk=P["ffp"] // 4)
        g_next = W["norm_mix_pre"][l + 1] if l + 1 < depth else None
        x, h = resnorm(x, y, W["norm_ffn_post"][l], g_next)

    st = lambda xs: jnp.stack(xs, 0)
    return (x.reshape(nb, t, d), st(lats), st(kpes), st(convs), st(bks), st(bvs),
            st(mks) if prompt else None, st(mvs) if prompt else None)


def kernel(x_prompt, x_sample, cache_mla_latent, cache_mla_kpe, state_conv, cache_band_k,
           cache_band_v, cache_mem_k, cache_mem_v, mem_prompt, norm_mix_pre, norm_mix_post,
           norm_mem_pre, norm_mem_post, norm_memkv, norm_ffn_pre, norm_ffn_post, w_in_ab,
           conv_w, conv_b, conv_ln_g, conv_ln_b, q_norm_g, w_uq, kv_norm_g, w_ukv, w_out_ab,
           w_qkv_c, rel_bias_c, w_out_c, w_q_mem, w_kv_mem, w_o_mem, w_gate_up, w_down):
    W = dict(norm_mix_pre=norm_mix_pre, norm_mix_post=norm_mix_post, norm_mem_pre=norm_mem_pre,
             norm_mem_post=norm_mem_post, norm_memkv=norm_memkv, norm_ffn_pre=norm_ffn_pre,
             norm_ffn_post=norm_ffn_post, w_in_ab=w_in_ab, conv_w=conv_w, conv_b=conv_b,
             conv_ln_g=conv_ln_g, conv_ln_b=conv_ln_b, q_norm_g=q_norm_g, w_uq=w_uq,
             kv_norm_g=kv_norm_g, w_ukv=w_ukv, w_out_ab=w_out_ab, w_qkv_c=w_qkv_c,
             rel_bias_c=rel_bias_c, w_out_c=w_out_c, w_q_mem=w_q_mem, w_kv_mem=w_kv_mem,
             w_o_mem=w_o_mem, w_gate_up=w_gate_up, w_down=w_down)
    ch = conv_w.shape[2]
    ql = q_norm_g.shape[1]
    kvl = kv_norm_g.shape[1]
    rope = cache_mla_kpe.shape[3]
    c_heads, c_dh = cache_band_k.shape[3], cache_band_k.shape[4]
    mem_heads, mem_dh = cache_mem_k.shape[3], cache_mem_k.shape[4]
    a_cols, b_cols, v_cols = w_uq.shape[2], w_ukv.shape[2], w_out_ab.shape[1] - ch
    mla_heads = (a_cols - b_cols + v_cols) // rope
    dims = dict(ch=ch, ql=ql, kvl=kvl, rope=rope, mla_heads=mla_heads,
                nope=(b_cols - v_cols) // mla_heads, dv=v_cols // mla_heads,
                c_heads=c_heads, c_dh=c_dh, mem_heads=mem_heads, mem_dh=mem_dh)
    assert dims["nope"] == LANE and dims["dv"] == LANE and rope == LANE // 2

    P = _prep_weights(W, dims)
    bias_full = jnp.stack([jnp.transpose(band_bias(rel_bias_c[o]), (1, 0, 2))
                           for o in range(rel_bias_c.shape[0])], 0)

    past = cache_mla_latent.shape[2]
    y_p, lat_p, kpe_p, conv_p, bk_p, bv_p, mk_p, mv_p = _trunk(
        x_prompt, 0, mem_prompt, None, None, None, None, None, None, None, W, P, dims, bias_full)
    y_s, lat_s, kpe_s, conv_s, bk_s, bv_s, _, _ = _trunk(
        x_sample, past, None, cache_mem_k, cache_mem_v, cache_mla_latent, cache_mla_kpe,
        state_conv, cache_band_k, cache_band_v, W, P, dims, bias_full)
    return (y_p, y_s, lat_p, kpe_p, conv_p, bk_p, bv_p, mk_p, mv_p, lat_s, kpe_s, conv_s, bk_s, bv_s)
```

```python
import functools

import jax
import jax.numpy as jnp
from jax import lax
from jax.experimental import pallas as pl
from jax.experimental.pallas import tpu as pltpu

F32 = jnp.float32
BF16 = jnp.bfloat16

EPS = 1e-6
NEG = -1e30
CHUNK = 64
BAND_CHUNKS = 8
ROPE_THETA = 10000.0
LOG2E = 1.4426950408889634

LANE = 128
MXU_DIM = 256
VMEM_LIMIT_BYTES = 56 * 1024 * 1024
FF_BLOCK = 512

BAND_QB = 4 * CHUNK
BAND_KB = BAND_CHUNKS * CHUNK + BAND_QB


def _cparams(*sem):
    return pltpu.CompilerParams(dimension_semantics=sem, vmem_limit_bytes=VMEM_LIMIT_BYTES)


def _blk(n, pref):
    if n <= pref:
        return n
    b = pref
    while n % b:
        b //= 2
    assert b >= 8, (n, pref)
    return b


def _rms_f32(x, g):
    return x * lax.rsqrt(jnp.mean(x * x, axis=-1, keepdims=True) + EPS) * g


def _rmsnorm_kernel(x_ref, g_ref, o_ref):
    o_ref[...] = _rms_f32(x_ref[...].astype(F32), g_ref[...]).astype(o_ref.dtype)


def rmsnorm(x, g, out_dtype=BF16):
    m, d = x.shape
    bm = _blk(m, 256)
    return pl.pallas_call(
        _rmsnorm_kernel,
        grid=(m // bm,),
        in_specs=[pl.BlockSpec((bm, d), lambda i: (i, 0)), pl.BlockSpec((1, d), lambda i: (0, 0))],
        out_specs=pl.BlockSpec((bm, d), lambda i: (i, 0)),
        out_shape=jax.ShapeDtypeStruct((m, d), out_dtype),
        compiler_params=_cparams("parallel"),
        name="rmsnorm",
    )(x, g.reshape(1, d))


def _resnorm_kernel(x_ref, y_ref, gp_ref, gn_ref, xo_ref, h_ref):
    xn = x_ref[...] + _rms_f32(y_ref[...], gp_ref[...])
    xo_ref[...] = xn
    h_ref[...] = _rms_f32(xn, gn_ref[...]).astype(h_ref.dtype)


def _resnorm_last_kernel(x_ref, y_ref, gp_ref, xo_ref):
    xo_ref[...] = x_ref[...] + _rms_f32(y_ref[...], gp_ref[...])


def resnorm(x, y, g_post, g_next):
    m, d = x.shape
    bm = _blk(m, 256)
    row = pl.BlockSpec((bm, d), lambda i: (i, 0))
    vec = pl.BlockSpec((1, d), lambda i: (0, 0))
    if g_next is None:
        return pl.pallas_call(
            _resnorm_last_kernel, grid=(m // bm,), in_specs=[row, row, vec], out_specs=row,
            out_shape=jax.ShapeDtypeStruct((m, d), F32), compiler_params=_cparams("parallel"),
            name="resnorm_last",
        )(x, y, g_post.reshape(1, d)), None
    return pl.pallas_call(
        _resnorm_kernel, grid=(m // bm,), in_specs=[row, row, vec, vec], out_specs=[row, row],
        out_shape=[jax.ShapeDtypeStruct((m, d), F32), jax.ShapeDtypeStruct((m, d), BF16)],
        compiler_params=_cparams("parallel"), name="resnorm",
    )(x, y, g_post.reshape(1, d), g_next.reshape(1, d))


def _mm_kernel(a_ref, w_ref, o_ref, *, scale, scale_blocks):
    acc = jnp.dot(a_ref[...], w_ref[...], preferred_element_type=F32)
    if scale is not None:
        sc = scale if scale_blocks is None else jnp.where(pl.program_id(1) < scale_blocks, scale, 1.0)
        acc = acc * sc
    o_ref[...] = acc.astype(o_ref.dtype)


def matmul(a, w, out_dtype, *, n_off=0, n=None, scale=None, scale_cols=None, bm_pref=1024, bn_pref=1024):
    m, k = a.shape
    n = w.shape[1] - n_off if n is None else n
    bm = _blk(m, bm_pref)
    bn = _blk(n, bn_pref)
    while n_off % bn:
        bn //= 2
    assert bn % LANE == 0 or bn == w.shape[1]
    joff = n_off // bn
    scale_blocks = None
    if scale_cols is not None:
        assert scale_cols % bn == 0
        scale_blocks = scale_cols // bn
    return pl.pallas_call(
        functools.partial(_mm_kernel, scale=scale, scale_blocks=scale_blocks),
        grid=(m // bm, n // bn),
        in_specs=[pl.BlockSpec((bm, k), lambda i, j: (i, 0)),
                  pl.BlockSpec((k, bn), lambda i, j: (0, j + joff))],
        out_specs=pl.BlockSpec((bm, bn), lambda i, j: (i, j)),
        out_shape=jax.ShapeDtypeStruct((m, n), out_dtype),
        compiler_params=_cparams("parallel", "parallel"),
        name="matmul",
    )(a, w)


def _mm2_kernel(a1_ref, a2_ref, w1_ref, w2_ref, o_ref):
    acc = jnp.dot(a1_ref[...], w1_ref[...], preferred_element_type=F32)
    acc = acc + jnp.dot(a2_ref[...], w2_ref[...], preferred_element_type=F32)
    o_ref[...] = acc.astype(o_ref.dtype)


def matmul_cat2(a1, a2, w, out_dtype):
    m, k1 = a1.shape
    assert a2.shape == (m, k1) and w.shape[0] == 2 * k1
    n = w.shape[1]
    bm = _blk(m, 1024)
    bn = _blk(n, 1024)
    return pl.pallas_call(
        _mm2_kernel,
        grid=(m // bm, n // bn),
        in_specs=[pl.BlockSpec((bm, k1), lambda i, j: (i, 0)),
                  pl.BlockSpec((bm, k1), lambda i, j: (i, 0)),
                  pl.BlockSpec((k1, bn), lambda i, j: (0, j)),
                  pl.BlockSpec((k1, bn), lambda i, j: (1, j))],
        out_specs=pl.BlockSpec((bm, bn), lambda i, j: (i, j)),
        out_shape=jax.ShapeDtypeStruct((m, n), out_dtype),
        compiler_params=_cparams("parallel", "parallel"),
        name="matmul_cat2",
    )(a1, a2, w, w)


def _mm_acc_kernel(a_ref, w_ref, o_ref, acc_ref, *, nk, bk, k_valid):
    kk = pl.program_id(2)
    w = w_ref[...]
    if k_valid < nk * bk:
        row = lax.broadcasted_iota(jnp.int32, w.shape, 0)
        w = jnp.where(row < k_valid - kk * bk, w, jnp.zeros_like(w))
    part = jnp.dot(a_ref[...], w, preferred_element_type=F32)

    @pl.when(kk == 0)
    def _():
        acc_ref[...] = part

    @pl.when(kk > 0)
    def _():
        acc_ref[...] += part

    @pl.when(kk == nk - 1)
    def _():
        o_ref[...] = acc_ref[...].astype(o_ref.dtype)


def matmul_ktiled(a, w, out_dtype, *, bk):
    m, k = a.shape
    k_valid, n = w.shape
    bm = _blk(m, 1024)
    bn = _blk(n, 1024)
    assert k % bk == 0 and k - bk < k_valid <= k
    nk = k // bk
    return pl.pallas_call(
        functools.partial(_mm_acc_kernel, nk=nk, bk=bk, k_valid=k_valid),
        grid=(m // bm, n // bn, nk),
        in_specs=[pl.BlockSpec((bm, bk), lambda i, j, kk: (i, kk)),
                  pl.BlockSpec((bk, bn), lambda i, j, kk: (kk, j))],
        out_specs=pl.BlockSpec((bm, bn), lambda i, j, kk: (i, j)),
        out_shape=jax.ShapeDtypeStruct((m, n), out_dtype),
        scratch_shapes=[pltpu.VMEM((bm, bn), F32)],
        compiler_params=_cparams("parallel", "parallel", "arbitrary"),
        name="matmul_ktiled",
    )(a, w)


def _swiglu_kernel(a_ref, wg_ref, wu_ref, o_ref, *, bn, ff):
    a = a_ref[...]
    g = jnp.dot(a, wg_ref[...], preferred_element_type=F32)
    u = jnp.dot(a, wu_ref[...], preferred_element_type=F32)
    act = g * jax.nn.sigmoid(g) * u
    col = pl.program_id(1) * bn + lax.broadcasted_iota(jnp.int32, (1, bn), 1)
    o_ref[...] = jnp.where(col < ff, act, 0.0).astype(o_ref.dtype)


def swiglu_up(h, w_gate, w_up, *, ffp):
    m, k = h.shape
    ff = w_gate.shape[1]
    bm = _blk(m, 1024)
    bn = FF_BLOCK
    assert ffp % bn == 0 and ffp - bn < ff <= ffp
    return pl.pallas_call(
        functools.partial(_swiglu_kernel, bn=bn, ff=ff),
        grid=(m // bm, ffp // bn),
        in_specs=[pl.BlockSpec((bm, k), lambda i, j: (i, 0)),
                  pl.BlockSpec((k, bn), lambda i, j: (0, j)),
                  pl.BlockSpec((k, bn), lambda i, j: (0, j))],
        out_specs=pl.BlockSpec((bm, bn), lambda i, j: (i, j)),
        out_shape=jax.ShapeDtypeStruct((m, ffp), BF16),
        compiler_params=_cparams("parallel", "parallel"),
        name="swiglu_up",
    )(h, w_gate, w_up)


def _conv_kernel(a_ref, b_ref, ah_ref, bh_ref, hist_ref, cw_ref, cb_ref, lg_ref, lb_ref, o_ref,
                 buf_ref, y_ref, *, tb, width, halo, tt):
    t_blk = pl.program_id(1)
    a = a_ref[...].astype(F32)
    u_cur = a * jax.nn.sigmoid(b_ref[...].astype(F32))
    @pl.when(t_blk == 0)
    def _():
        buf_ref[0:halo, :] = hist_ref[...]

    @pl.when(t_blk > 0)
    def _():
        ah = ah_ref[...].astype(F32)
        buf_ref[0:halo, :] = ah * jax.nn.sigmoid(bh_ref[...].astype(F32))

    buf_ref[halo:halo + tb, :] = u_cur
    ch = a.shape[1]
    base = halo - (width - 1)

    def lane_tile(ct, carry):
        c0 = pl.multiple_of(ct * LANE, LANE)
        bias = cb_ref[:, pl.ds(c0, LANE)]
        for t0 in range(0, tb, tt):
            acc = jnp.broadcast_to(bias, (tt, LANE))
            for w in range(width):
                acc = acc + cw_ref[w:w + 1, pl.ds(c0, LANE)] * buf_ref[pl.ds(base + t0 + w, tt), pl.ds(c0, LANE)]
            y_ref[pl.ds(t0, tt), pl.ds(c0, LANE)] = acc
        return carry

    lax.fori_loop(0, ch // LANE, lane_tile, 0)
    y = y_ref[...]
    mu = jnp.mean(y, axis=-1, keepdims=True)
    yc = y - mu
    var = jnp.mean(yc * yc, axis=-1, keepdims=True)
    z = yc * lax.rsqrt(var + EPS) * lg_ref[...] + lb_ref[...]
    o_ref[...] = (z * jax.nn.sigmoid(z)).astype(o_ref.dtype)


def conv_module(z_main, hist, conv_w, conv_b, ln_g, ln_b, *, nb, t, ch):
    width = conv_w.shape[0]
    halo = hist.shape[1]
    tb = _blk(t, 256)
    tt = _blk(tb, 64)
    nt = t // tb
    hb = tb // halo
    cw = jnp.pad(conv_w, ((0, halo - width), (0, 0)))
    row = lambda c: pl.BlockSpec((tb, ch), lambda b, i: (b * nt + i, c))
    prev = lambda c: pl.BlockSpec((halo, ch), lambda b, i: (jnp.maximum((b * nt + i) * hb - 1, 0), c))
    vec = pl.BlockSpec((1, ch), lambda b, i: (0, 0))
    return pl.pallas_call(
        functools.partial(_conv_kernel, tb=tb, width=width, halo=halo, tt=tt),
        grid=(nb, nt),
        in_specs=[row(0), row(1), prev(0), prev(1),
                  pl.BlockSpec((None, halo, ch), lambda b, i: (b, 0, 0)),
                  pl.BlockSpec((halo, ch), lambda b, i: (0, 0)), vec, vec, vec],
        out_specs=pl.BlockSpec((tb, ch), lambda b, i: (b * nt + i, 0)),
        out_shape=jax.ShapeDtypeStruct((nb * t, ch), BF16),
        scratch_shapes=[pltpu.VMEM((halo + tb, ch), F32), pltpu.VMEM((tb, ch), F32)],
        compiler_params=_cparams("parallel", "parallel"),
        name="conv_module",
    )(z_main, z_main, z_main, z_main, hist, cw, conv_b.reshape(1, ch), ln_g.reshape(1, ch),
      ln_b.reshape(1, ch))


def _glu_tail_kernel(z_ref, o_ref, *, ch, keep):
    z = z_ref[...]
    u = z[:, :ch] * jax.nn.sigmoid(z[:, ch:])
    o_ref[...] = u[u.shape[0] - keep:, :]


def glu_tail(z_tail, *, nb, rows, ch, keep):
    return pl.pallas_call(
        functools.partial(_glu_tail_kernel, ch=ch, keep=keep),
        grid=(nb,),
        in_specs=[pl.BlockSpec((rows, 2 * ch), lambda b: (b, 0))],
        out_specs=pl.BlockSpec((None, keep, ch), lambda b: (b, 0, 0)),
        out_shape=jax.ShapeDtypeStruct((nb, keep, ch), F32),
        compiler_params=_cparams("parallel"),
        name="glu_tail",
    )(z_tail)


def _rope_pairs(v, cc, ss):
    return v * cc + pltpu.roll(v, LANE // 2, axis=1) * ss


def _mla_kv_prep_kernel(z_ref, g_ref, cc_ref, ss_ref, lat_ref, kpe_ref, a_ref, *, kvl):
    z = z_ref[...]
    lat = _rms_f32(z[:, :kvl], g_ref[...])
    rot = _rope_pairs(z[:, kvl:kvl + LANE], cc_ref[...], ss_ref[...])
    lat_ref[...] = lat
    kpe_ref[...] = rot[:, :LANE // 2]
    a_ref[:, :kvl] = lat.astype(a_ref.dtype)
    a_ref[:, kvl:] = rot.astype(a_ref.dtype)


def mla_kv_prep(z_kv, g, cc, ss, *, t):
    m, wdt = z_kv.shape
    kvl = wdt - LANE
    bm = _blk(t, 256)
    nt = t // bm
    row = lambda wd: pl.BlockSpec((bm, wd), lambda i: (i, 0))
    tab = pl.BlockSpec((bm, LANE), lambda i: (i % nt, 0))
    return pl.pallas_call(
        functools.partial(_mla_kv_prep_kernel, kvl=kvl),
        grid=(m // bm,),
        in_specs=[row(wdt), pl.BlockSpec((1, kvl), lambda i: (0, 0)), tab, tab],
        out_specs=[row(kvl), row(LANE // 2), row(wdt)],
        out_shape=[jax.ShapeDtypeStruct((m, kvl), F32), jax.ShapeDtypeStruct((m, LANE // 2), F32),
                   jax.ShapeDtypeStruct((m, wdt), BF16)],
        compiler_params=_cparams("parallel"),
        name="mla_kv_prep",
    )(z_kv, g.reshape(1, kvl), cc, ss)


def _mla_q_kernel(z_ref, g_ref, w_ref, cc_ref, ss_ref, o_ref, *, heads, scale):
    a = _rms_f32(z_ref[...].astype(F32), g_ref[...]).astype(BF16)
    acc = jnp.dot(a, w_ref[...], preferred_element_type=F32)
    cc = cc_ref[...]
    ss = ss_ref[...]
    for h in range(heads):
        c0 = h * MXU_DIM
        o_ref[:, c0:c0 + LANE] = (acc[:, c0:c0 + LANE] * scale).astype(o_ref.dtype)
        rot = _rope_pairs(acc[:, c0 + LANE:c0 + MXU_DIM], cc, ss)
        o_ref[:, c0 + LANE:c0 + MXU_DIM] = (rot * scale).astype(o_ref.dtype)


def mla_q_proj(z_main, col_blk, g, w_uq_ext, cc, ss, *, t, heads, scale):
    m = z_main.shape[0]
    ql, n = w_uq_ext.shape
    bm = _blk(t, 512)
    nt = t // bm
    tab = pl.BlockSpec((bm, LANE), lambda i: (i % nt, 0))
    return pl.pallas_call(
        functools.partial(_mla_q_kernel, heads=heads, scale=scale),
        grid=(m // bm,),
        in_specs=[pl.BlockSpec((bm, ql), lambda i: (i, col_blk)),
                  pl.BlockSpec((1, ql), lambda i: (0, 0)),
                  pl.BlockSpec((ql, n), lambda i: (0, 0)), tab, tab],
        out_specs=pl.BlockSpec((bm, n), lambda i: (i, 0)),
        out_shape=jax.ShapeDtypeStruct((m, n), BF16),
        compiler_params=_cparams("parallel"),
        name="mla_q_proj",
    )(z_main, g.reshape(1, ql), w_uq_ext, cc, ss)


def _mla_attn_kernel(q_ref, k_ref, v_ref, o_ref, m_ref, l_ref, acc_ref, *, heads, dv, bq, bk, nk,
                     mode, kv_len):
    i = pl.program_id(1)
    j = pl.program_id(2)

    @pl.when(j == 0)
    def _():
        m_ref[...] = jnp.full(m_ref.shape, NEG, F32)
        l_ref[...] = jnp.zeros(l_ref.shape, F32)
        acc_ref[...] = jnp.zeros(acc_ref.shape, F32)

    def step(mask):
        for h in range(heads):
            qh = q_ref[:, h * MXU_DIM:(h + 1) * MXU_DIM]
            kh = k_ref[:, h * MXU_DIM:(h + 1) * MXU_DIM]
            s = lax.dot_general(qh, kh, (((1,), (1,)), ((), ())), preferred_element_type=F32)
            if mask is not None:
                s = jnp.where(mask, s, NEG)
            tiles = [s[:, t * LANE:(t + 1) * LANE] for t in range(bk // LANE)]
            mx = tiles[0]
            for tl in tiles[1:]:
                mx = jnp.maximum(mx, tl)
            m_prev = m_ref[h]
            m_new = jnp.maximum(m_prev, jnp.max(mx, axis=1, keepdims=True))
            alpha = jnp.exp2(m_prev - m_new)
            ps = [jnp.exp2(tl - m_new) for tl in tiles]
            lsum = ps[0]
            for p in ps[1:]:
                lsum = lsum + p
            l_ref[h] = alpha * l_ref[h] + lsum
            pb = jnp.concatenate(ps, axis=1).astype(BF16)
            pv = jnp.dot(pb, v_ref[:, h * dv:(h + 1) * dv], preferred_element_type=F32)
            acc_ref[:, h * dv:(h + 1) * dv] = alpha * acc_ref[:, h * dv:(h + 1) * dv] + pv
            m_ref[h] = m_new

    if mode == "block_causal":
        @pl.when(j < i)
        def _():
            step(None)

        @pl.when(j == i)
        def _():
            qc = lax.broadcasted_iota(jnp.int32, (bq, bk), 0) // CHUNK
            kc = lax.broadcasted_iota(jnp.int32, (bq, bk), 1) // CHUNK
            step(kc <= qc)
    else:
        @pl.when(j < nk - 1)
        def _():
            step(None)

        @pl.when(j == nk - 1)
        def _():
            kpos = (nk - 1) * bk + lax.broadcasted_iota(jnp.int32, (bq, bk), 1)
            step(kpos < kv_len if kv_len < nk * bk else None)

    @pl.when(j == nk - 1)
    def _():
        for h in range(heads):
            l = jnp.sum(l_ref[h], axis=1, keepdims=True)
            o_ref[:, h * dv:(h + 1) * dv] = (acc_ref[:, h * dv:(h + 1) * dv] / l).astype(o_ref.dtype)


def mla_attention(q, kv, *, nb, tq, lk, kv_len, heads, dv, mode):
    kw = heads * MXU_DIM
    vw = heads * dv
    assert kw % vw == 0 and dv == LANE
    bq = _blk(tq, 512)
    if mode == "block_causal":
        assert lk == tq and kv_len == lk and bq % CHUNK == 0
        bk = bq
    else:
        bk = _blk(lk, 768)
    nq, nk = tq // bq, lk // bk
    assert kv_len > (nk - 1) * bk, "key padding must sit in the last key block"
    if mode == "block_causal":
        kidx = lambda b, i, j: (b * nk + jnp.minimum(j, i), 0)
        vidx = lambda b, i, j: (b * nk + jnp.minimum(j, i), kw // vw)
    else:
        kidx = lambda b, i, j: (b * nk + j, 0)
        vidx = lambda b, i, j: (b * nk + j, kw // vw)
    return pl.pallas_call(
        functools.partial(_mla_attn_kernel, heads=heads, dv=dv, bq=bq, bk=bk, nk=nk, mode=mode,
                          kv_len=kv_len),
        grid=(nb, nq, nk),
        in_specs=[pl.BlockSpec((bq, kw), lambda b, i, j: (b * nq + i, 0)),
                  pl.BlockSpec((bk, kw), kidx),
                  pl.BlockSpec((bk, vw), vidx)],
        out_specs=pl.BlockSpec((bq, vw), lambda b, i, j: (b * nq + i, 0)),
        out_shape=jax.ShapeDtypeStruct((nb * tq, vw), BF16),
        scratch_shapes=[pltpu.VMEM((heads, bq, LANE), F32), pltpu.VMEM((heads, bq, LANE), F32),
                        pltpu.VMEM((bq, vw), F32)],
        compiler_params=_cparams("parallel", "parallel", "arbitrary"),
        name="mla_attention",
    )(q, kv, kv)


def _band_bias_kernel(t_ref, o_ref, *, rows, max_rel, kb, past):
    t = t_ref[...]
    hi = t.astype(BF16)
    r1 = t - hi.astype(F32)
    mid = r1.astype(BF16)
    lo = (r1 - mid.astype(F32)).astype(BF16)
    nrel = t.shape[1]
    j = lax.broadcasted_iota(jnp.int32, (nrel, kb), 1)
    r = lax.broadcasted_iota(jnp.int32, (nrel, kb), 0)
    kc = lax.broadcasted_iota(jnp.int32, (t.shape[0], kb), 1) // CHUNK - past // CHUNK
    for rr in range(rows):
        i = pl.program_id(0) * rows + rr
        idx = jnp.clip(i - j + past, -max_rel, max_rel) + max_rel
        onehot = jnp.where(r == idx, 1.0, 0.0).astype(BF16)
        vals = (jnp.dot(hi, onehot, preferred_element_type=F32)
                + jnp.dot(mid, onehot, preferred_element_type=F32)
                + jnp.dot(lo, onehot, preferred_element_type=F32))
        qc = i // CHUNK
        ok = (kc <= qc) & (kc >= qc - BAND_CHUNKS)
        o_ref[rr] = jnp.where(ok, vals * LOG2E, NEG)


def band_bias(table):
    heads, nrel = table.shape
    max_rel = (nrel - 1) // 2
    nrel_p = -(-nrel // LANE) * LANE
    tp = jnp.pad(table, ((0, 0), (0, nrel_p - nrel)))
    rows = 8
    past = BAND_CHUNKS * CHUNK
    return pl.pallas_call(
        functools.partial(_band_bias_kernel, rows=rows, max_rel=max_rel, kb=BAND_KB, past=past),
        grid=(BAND_QB // rows,),
        in_specs=[pl.BlockSpec((heads, nrel_p), lambda i: (0, 0))],
        out_specs=pl.BlockSpec((rows, heads, BAND_KB), lambda i: (i, 0, 0)),
        out_shape=jax.ShapeDtypeStruct((BAND_QB, heads, BAND_KB), F32),
        compiler_params=_cparams("parallel"),
        name="band_bias",
    )(tp)


def _softmax_pv(tiles, v_parts):
    mx = tiles[0]
    for tl in tiles[1:]:
        mx = jnp.maximum(mx, tl)
    m = jnp.max(mx, axis=1, keepdims=True)
    ps = [jnp.exp2(tl - m) for tl in tiles]
    lsum = ps[0]
    for p in ps[1:]:
        lsum = lsum + p
    l = jnp.sum(lsum, axis=1, keepdims=True)
    per = len(tiles) // len(v_parts)
    o = None
    for n, vp in enumerate(v_parts):
        pb = jnp.concatenate(ps[n * per:(n + 1) * per], axis=1).astype(BF16)
        part = jnp.dot(pb, vp, preferred_element_type=F32)
        o = part if o is None else o + part
    return o / l


def _lane_tiles(s):
    return [s[:, c:c + LANE] for c in range(0, s.shape[1], LANE)]


def _band_prompt_kernel(q_ref, k0_ref, k1_ref, k2_ref, v0_ref, v1_ref, v2_ref, bias_ref, o_ref, *,
                        group, dh):
    qb = pl.program_id(2)
    pen = [jnp.where(qb < 2, NEG, 0.0), jnp.where(qb < 1, NEG, 0.0), None]
    for g in range(group):
        sl = slice(g * dh, (g + 1) * dh)
        qh = q_ref[:, sl]
        tiles = []
        for n, kr in enumerate((k0_ref, k1_ref, k2_ref)):
            part = lax.dot_general(qh, kr[:, sl], (((1,), (1,)), ((), ())), preferred_element_type=F32)
            for c in range(0, BAND_QB, LANE):
                tl = part[:, c:c + LANE] + bias_ref[g, :, n * BAND_QB + c:n * BAND_QB + c + LANE]
                tiles.append(tl if pen[n] is None else tl + pen[n])
        o = _softmax_pv(tiles, [v0_ref[:, sl], v1_ref[:, sl], v2_ref[:, sl]])
        o_ref[:, sl] = o.astype(o_ref.dtype)


def band_attention_prompt(qkv, bias, *, nb, t, heads, dh):
    d = heads * dh
    group = min(heads, 8)
    ng = heads // group
    gw = group * dh
    nq = t // BAND_QB
    assert t % BAND_QB == 0 and BAND_KB == 3 * BAND_QB

    def kspec(back, col0):
        return pl.BlockSpec((BAND_QB, gw),
                            lambda hg, b, i: (b * nq + jnp.maximum(i - back, 0), col0 + hg))

    return pl.pallas_call(
        functools.partial(_band_prompt_kernel, group=group, dh=dh),
        grid=(ng, nb, nq),
        in_specs=[pl.BlockSpec((BAND_QB, gw), lambda hg, b, i: (b * nq + i, hg)),
                  kspec(2, ng), kspec(1, ng), kspec(0, ng),
                  kspec(2, 2 * ng), kspec(1, 2 * ng), kspec(0, 2 * ng),
                  pl.BlockSpec((group, BAND_QB, BAND_KB), lambda hg, b, i: (hg, 0, 0))],
        out_specs=pl.BlockSpec((BAND_QB, gw), lambda hg, b, i: (b * nq + i, hg)),
        out_shape=jax.ShapeDtypeStruct((nb * t, d), BF16),
        compiler_params=_cparams("parallel", "parallel", "parallel"),
        name="band_attention_prompt",
    )(qkv, qkv, qkv, qkv, qkv, qkv, qkv, bias)


def _band_step_kernel(q_ref, k_ref, v_ref, bias_ref, o_ref, *, group, dh):
    for g in range(group):
        sl = slice(g * dh, (g + 1) * dh)
        s = lax.dot_general(q_ref[:, sl], k_ref[:, sl], (((1,), (1,)), ((), ())),
                            preferred_element_type=F32) + bias_ref[g]
        o = _softmax_pv(_lane_tiles(s), [v_ref[:, sl]])
        o_ref[:, sl] = o.astype(o_ref.dtype)


def band_attention_step(qkv, k_all, v_all, bias, *, nb, t, heads, dh):
    d = heads * dh
    group = min(heads, 8)
    ng = heads // group
    gw = group * dh
    lk = k_all.shape[0] // nb
    return pl.pallas_call(
        functools.partial(_band_step_kernel, group=group, dh=dh),
        grid=(ng, nb),
        in_specs=[pl.BlockSpec((t, gw), lambda hg, b: (b, hg)),
                  pl.BlockSpec((lk, gw), lambda hg, b: (b, hg)),
                  pl.BlockSpec((lk, gw), lambda hg, b: (b, hg)),
                  pl.BlockSpec((group, t, lk), lambda hg, b: (hg, 0, 0))],
        out_specs=pl.BlockSpec((t, gw), lambda hg, b: (b, hg)),
        out_shape=jax.ShapeDtypeStruct((nb * t, d), BF16),
        compiler_params=_cparams("parallel", "parallel"),
        name="band_attention_step",
    )(qkv, k_all, v_all, bias)


def _mem_attn_kernel(q_ref, k_ref, v_ref, o_ref, *, heads, dh):
    for h in range(heads):
        sl = slice(h * dh, (h + 1) * dh)
        s = lax.dot_general(q_ref[:, sl], k_ref[:, sl], (((1,), (1,)), ((), ())),
                            preferred_element_type=F32)
        o = _softmax_pv(_lane_tiles(s), [v_ref[:, sl]])
        o_ref[:, sl] = o.astype(o_ref.dtype)


def mem_attention(q, mk, mv, *, nb, t, heads, dh):
    d = heads * dh
    ml = mk.shape[0] // nb
    bq = _blk(t, 512)
    nq = t // bq
    return pl.pallas_call(
        functools.partial(_mem_attn_kernel, heads=heads, dh=dh),
        grid=(nb, nq),
        in_specs=[pl.BlockSpec((bq, d), lambda b, i: (b * nq + i, 0)),
                  pl.BlockSpec((ml, d), lambda b, i: (b, 0)),
                  pl.BlockSpec((ml, d), lambda b, i: (b, 0))],
        out_specs=pl.BlockSpec((bq, d), lambda b, i: (b * nq + i, 0)),
        out_shape=jax.ShapeDtypeStruct((nb * t, d), BF16),
        compiler_params=_cparams("parallel", "parallel"),
        name="mem_attention",
    )(q, mk, mv)


def _prep_weights(W, dims):
    heads, nope, rope, dv, kvl = dims["mla_heads"], dims["nope"], dims["rope"], dims["dv"], dims["kvl"]
    ch, ql = dims["ch"], dims["ql"]
    half = rope // 2
    P = {}
    w_in = W["w_in_ab"]
    s2 = 2 * ch + ql
    P["w_in_main"] = w_in[:, :, :s2].astype(BF16)
    kpe0 = s2 + kvl
    P["w_in_kv"] = jnp.concatenate(
        [w_in[:, :, s2:], w_in[:, :, kpe0 + half:kpe0 + rope], w_in[:, :, kpe0:kpe0 + half]],
        axis=-1).astype(BF16)
    ne = w_in.shape[0]
    wq = W["w_uq"].reshape(ne, ql, heads, nope + rope)
    pe = wq[..., nope:]
    pe_sw = jnp.concatenate([pe[..., half:], pe[..., :half]], axis=-1)
    P["w_uq"] = jnp.concatenate([wq[..., :nope], pe, pe_sw], axis=-1).reshape(
        ne, ql, heads * MXU_DIM).astype(BF16)
    wkv = W["w_ukv"].reshape(ne, kvl, heads, nope + dv)
    kslot = jnp.pad(wkv[..., :nope], ((0, 0), (0, LANE), (0, 0), (0, MXU_DIM - nope)))
    eye = jnp.zeros((LANE, MXU_DIM), F32).at[jnp.arange(rope), nope + jnp.arange(rope)].set(1.0)
    kslot = kslot + jnp.pad(eye, ((kvl, 0), (0, 0)))[None, :, None, :]
    vpart = jnp.pad(wkv[..., nope:], ((0, 0), (0, LANE), (0, 0), (0, 0)))
    P["w_ukv"] = jnp.concatenate(
        [kslot.reshape(ne, kvl + LANE, heads * MXU_DIM), vpart.reshape(ne, kvl + LANE, heads * dv)],
        axis=-1).astype(BF16)
    for name in ("w_out_ab", "w_qkv_c", "w_out_c", "w_q_mem", "w_kv_mem", "w_o_mem"):
        P[name] = W[name].astype(BF16)
    ff = W["w_down"].shape[1]
    P["w_gate"] = W["w_gate_up"][:, :, :ff].astype(BF16)
    P["w_up"] = W["w_gate_up"][:, :, ff:].astype(BF16)
    P["w_down"] = W["w_down"].astype(BF16)
    P["ffp"] = -(-ff // FF_BLOCK) * FF_BLOCK
    return P


def _rope_tabs(pos, rope):
    half = rope // 2
    inv = ROPE_THETA ** (-jnp.arange(0, rope, 2, dtype=F32) / rope)
    ang = pos.astype(F32)[:, None] * inv[None, :]
    cos, sin = jnp.cos(ang), jnp.sin(ang)
    zero = jnp.zeros((pos.shape[0], LANE - rope), F32)
    assert 2 * rope == LANE and half * 2 == rope
    return jnp.concatenate([cos, cos, zero], axis=1), jnp.concatenate([-sin, sin, zero], axis=1)


def _trunk(x3, pos0, mem, mem_k_c, mem_v_c, lat_c, kpe_c, conv_c, bk_c, bv_c, W, P, dims, bias_full):
    prompt = mem is not None
    nb, t, d = x3.shape
    m = nb * t
    depth = W["norm_mix_pre"].shape[0]
    ch, ql, kvl, rope = dims["ch"], dims["ql"], dims["kvl"], dims["rope"]
    mh, nope, dv = dims["mla_heads"], dims["nope"], dims["dv"]
    cheads, cdh = dims["c_heads"], dims["c_dh"]
    memh, memdh = dims["mem_heads"], dims["mem_dh"]
    width = W["conv_w"].shape[1]
    keep = width - 1
    halo = 32
    assert keep <= halo and t % halo == 0 and (2 * ch) % ql == 0
    band_keep = BAND_CHUNKS * CHUNK

    x = x3.reshape(m, d)
    cc, ss = _rope_tabs(pos0 + jnp.arange(t, dtype=jnp.int32), rope)
    q_scale = (nope + rope) ** -0.5 * LOG2E
    lats, kpes, convs, bks, bvs, mks, mvs = [], [], [], [], [], [], []

    def tail_rows(h, rows):
        return h.reshape(nb, t, d)[:, t - rows:].reshape(nb * rows, d)

    h = rmsnorm(x, W["norm_mix_pre"][0])
    for l in range(depth):
        if l % 2 == 0:
            e = l // 2
            z_main = matmul(h, P["w_in_main"][e], BF16)
            z_kv = matmul(h, P["w_in_kv"][e], F32)
            if prompt:
                hist = jnp.zeros((nb, halo, ch), F32)
            else:
                hist = jnp.pad(conv_c[e], ((0, 0), (halo - keep, 0), (0, 0)))
            ya = conv_module(z_main, hist, W["conv_w"][e], W["conv_b"][e], W["conv_ln_g"][e],
                             W["conv_ln_b"][e], nb=nb, t=t, ch=ch)
            z_tail = matmul(tail_rows(h, halo), P["w_in_main"][e], F32, n=2 * ch)
            convs.append(glu_tail(z_tail, nb=nb, rows=halo, ch=ch, keep=keep))
            lat, kpe, a_kv = mla_kv_prep(z_kv, W["kv_norm_g"][e], cc, ss, t=t)
            lats.append(lat.reshape(nb, t, kvl))
            kpes.append(kpe.reshape(nb, t, rope))
            q = mla_q_proj(z_main, (2 * ch) // ql, W["q_norm_g"][e], P["w_uq"][e], cc, ss, t=t,
                           heads=mh, scale=q_scale)
            if prompt:
                lk, kv_len, mode = t, t, "block_causal"
                a_all = a_kv
            else:
                past = lat_c.shape[2]
                kv_len = past + t
                assert (kv_len - 1) // CHUNK <= (past // CHUNK), "decode keys must all be visible"
                lk = -(-kv_len // (3 * LANE)) * (3 * LANE)
                a_hist = jnp.concatenate(
                    [lat_c[e], kpe_c[e], jnp.zeros((nb, past, LANE - rope), F32)], axis=-1).astype(BF16)
                a_all = jnp.concatenate(
                    [a_hist, a_kv.reshape(nb, t, kvl + LANE),
                     jnp.zeros((nb, lk - kv_len, kvl + LANE), BF16)], axis=1).reshape(nb * lk, kvl + LANE)
                mode = "all_visible"
            kv = matmul(a_all, P["w_ukv"][e], BF16)
            yb = mla_attention(q, kv, nb=nb, tq=t, lk=lk, kv_len=kv_len, heads=mh, dv=dv, mode=mode)
            y = matmul_cat2(ya, yb, P["w_out_ab"][e], F32)
        else:
            o = l // 2
            dc = cheads * cdh
            qkv = matmul(h, P["w_qkv_c"][o], BF16, scale=cdh ** -0.5 * LOG2E, scale_cols=dc)
            if prompt:
                yc = band_attention_prompt(qkv, bias_full[o], nb=nb, t=t, heads=cheads, dh=cdh)
                rows = min(band_keep, t)
            else:
                k_new = qkv[:, dc:2 * dc].reshape(nb, t, dc)
                v_new = qkv[:, 2 * dc:].reshape(nb, t, dc)
                lh = bk_c.shape[2]
                assert lh == band_keep and t == CHUNK and pos0 % CHUNK == 0
                lkp = -(-(lh + t) // LANE) * LANE
                zpad = jnp.zeros((nb, lkp - lh - t, dc), BF16)
                k_all = jnp.concatenate([bk_c[o].reshape(nb, lh, dc).astype(BF16), k_new, zpad], axis=1)
                v_all = jnp.concatenate([bv_c[o].reshape(nb, lh, dc).astype(BF16), v_new, zpad], axis=1)
                bias_s = jnp.pad(bias_full[o][:, :t, :lh + t], ((0, 0), (0, 0), (0, lkp - lh - t)),
                                 constant_values=NEG)
                yc = band_attention_step(qkv, k_all.reshape(nb * lkp, dc), v_all.reshape(nb * lkp, dc),
                                         bias_s, nb=nb, t=t, heads=cheads, dh=cdh)
                rows = t
            kv_tail = matmul(tail_rows(h, rows), P["w_qkv_c"][o], F32, n_off=dc, n=2 * dc)
            bks.append(kv_tail[:, :dc].reshape(nb, rows, cheads, cdh))
            bvs.append(kv_tail[:, dc:].reshape(nb, rows, cheads, cdh))
            y = matmul(yc, P["w_out_c"][o], F32)
        x, h = resnorm(x, y, W["norm_mix_post"][l], W["norm_mem_pre"][l])

        md = memh * memdh
        if prompt:
            ml = mem.shape[1]
            hm = rmsnorm(mem.reshape(nb * ml, d), W["norm_memkv"][l])
            mkv = matmul(hm, P["w_kv_mem"][l], F32)
            mk32, mv32 = mkv[:, :md], mkv[:, md:]
            mks.append(mk32.reshape(nb, ml, memh, memdh))
            mvs.append(mv32.reshape(nb, ml, memh, memdh))
        else:
            ml = mem_k_c.shape[2]
            mk32 = mem_k_c[l].reshape(nb * ml, md)
            mv32 = mem_v_c[l].reshape(nb * ml, md)
        qm = matmul(h, P["w_q_mem"][l], BF16, scale=memdh ** -0.5 * LOG2E)
        om = mem_attention(qm, mk32.astype(BF16), mv32.astype(BF16), nb=nb, t=t, heads=memh, dh=memdh)
        y = matmul(om, P["w_o_mem"][l], F32)
        x, h = resnorm(x, y, W["norm_mem_post"][l], W["norm_ffn_pre"][l])

        act = swiglu_up(h, P["w_gate"][l], P["w_up"][l], ffp=P["ffp"])
        y = matmul_ktiled(act, P["w_down"][l], F32, bk=P["ffp"] // 4)
        g_next = W["norm_mix_pre"][l + 1] if l + 1 < depth else None
        x, h = resnorm(x, y, W["norm_ffn_post"][l], g_next)

    st = lambda xs: jnp.stack(xs, 0)
    return (x.reshape(nb, t, d), st(lats), st(kpes), st(convs), st(bks), st(bvs),
            st(mks) if prompt else None, st(mvs) if prompt else None)


def kernel(x_prompt, x_sample, cache_mla_latent, cache_mla_kpe, state_conv, cache_band_k,
           cache_band_v, cache_mem_k, cache_mem_v, mem_prompt, norm_mix_pre, norm_mix_post,
           norm_mem_pre, norm_mem_post, norm_memkv, norm_ffn_pre, norm_ffn_post, w_in_ab,
           conv_w, conv_b, conv_ln_g, conv_ln_b, q_norm_g, w_uq, kv_norm_g, w_ukv, w_out_ab,
           w_qkv_c, rel_bias_c, w_out_c, w_q_mem, w_kv_mem, w_o_mem, w_gate_up, w_down):
    W = dict(norm_mix_pre=norm_mix_pre, norm_mix_post=norm_mix_post, norm_mem_pre=norm_mem_pre,
             norm_mem_post=norm_mem_post, norm_memkv=norm_memkv, norm_ffn_pre=norm_ffn_pre,
             norm_ffn_post=norm_ffn_post, w_in_ab=w_in_ab, conv_w=conv_w, conv_b=conv_b,
             conv_ln_g=conv_ln_g, conv_ln_b=conv_ln_b, q_norm_g=q_norm_g, w_uq=w_uq,
             kv_norm_g=kv_norm_g, w_ukv=w_ukv, w_out_ab=w_out_ab, w_qkv_c=w_qkv_c,
             rel_bias_c=rel_bias_c, w_out_c=w_out_c, w_q_mem=w_q_mem, w_kv_mem=w_kv_mem,
             w_o_mem=w_o_mem, w_gate_up=w_gate_up, w_down=w_down)
    ch = conv_w.shape[2]
    ql = q_norm_g.shape[1]
    kvl = kv_norm_g.shape[1]
    rope = cache_mla_kpe.shape[3]
    c_heads, c_dh = cache_band_k.shape[3], cache_band_k.shape[4]
    mem_heads, mem_dh = cache_mem_k.shape[3], cache_mem_k.shape[4]
    a_cols, b_cols, v_cols = w_uq.shape[2], w_ukv.shape[2], w_out_ab.shape[1] - ch
    mla_heads = (a_cols - b_cols + v_cols) // rope
    dims = dict(ch=ch, ql=ql, kvl=kvl, rope=rope, mla_heads=mla_heads,
                nope=(b_cols - v_cols) // mla_heads, dv=v_cols // mla_heads,
                c_heads=c_heads, c_dh=c_dh, mem_heads=mem_heads, mem_dh=mem_dh)
    assert dims["nope"] == LANE and dims["dv"] == LANE and rope == LANE // 2

    P = _prep_weights(W, dims)
    bias_full = jnp.stack([jnp.transpose(band_bias(rel_bias_c[o]), (1, 0, 2))
                           for o in range(rel_bias_c.shape[0])], 0)

    past = cache_mla_latent.shape[2]
    y_p, lat_p, kpe_p, conv_p, bk_p, bv_p, mk_p, mv_p = _trunk(
        x_prompt, 0, mem_prompt, None, None, None, None, None, None, None, W, P, dims, bias_full)
    y_s, lat_s, kpe_s, conv_s, bk_s, bv_s, _, _ = _trunk(
        x_sample, past, None, cache_mem_k, cache_mem_v, cache_mla_latent, cache_mla_kpe,
        state_conv, cache_band_k, cache_band_v, W, P, dims, bias_full)
    return (y_p, y_s, lat_p, kpe_p, conv_p, bk_p, bv_p, mk_p, mv_p, lat_s, kpe_s, conv_s, bk_s, bv_s)
```

```python
import functools

import jax
import jax.numpy as jnp
from jax import lax
from jax.experimental import pallas as pl
from jax.experimental.pallas import tpu as pltpu

F32 = jnp.float32
BF16 = jnp.bfloat16
Y_DTYPE = BF16

EPS = 1e-6
NEG = -1e30
CHUNK = 64
BAND_CHUNKS = 8
ROPE_THETA = 10000.0
LOG2E = 1.4426950408889634

LANE = 128
SUBLANES = 8
MXU_DIM = 256
VMEM_LIMIT_BYTES = 56 * 1024 * 1024
FF_BLOCK = 512
CAST_BLOCK_BYTES = 8 * 1024 * 1024

BAND_QB = 4 * CHUNK
BAND_KB = BAND_CHUNKS * CHUNK + BAND_QB


def _cparams(*sem):
    return pltpu.CompilerParams(dimension_semantics=sem, vmem_limit_bytes=VMEM_LIMIT_BYTES)


def _blk(n, pref):
    if n <= pref:
        return n
    b = pref
    while n % b:
        b //= 2
    assert b >= 8, (n, pref)
    return b


def _rms_f32(x, g):
    return x * lax.rsqrt(jnp.mean(x * x, axis=-1, keepdims=True) + EPS) * g


def _rmsnorm_kernel(x_ref, g_ref, o_ref):
    o_ref[...] = _rms_f32(x_ref[...].astype(F32), g_ref[...]).astype(o_ref.dtype)


def rmsnorm(x, g, out_dtype=BF16):
    m, d = x.shape
    bm = _blk(m, 256)
    return pl.pallas_call(
        _rmsnorm_kernel,
        grid=(m // bm,),
        in_specs=[pl.BlockSpec((bm, d), lambda i: (i, 0)), pl.BlockSpec((1, d), lambda i: (0, 0))],
        out_specs=pl.BlockSpec((bm, d), lambda i: (i, 0)),
        out_shape=jax.ShapeDtypeStruct((m, d), out_dtype),
        compiler_params=_cparams("parallel"),
        name="rmsnorm",
    )(x, g.reshape(1, d))


def _resnorm_kernel(x_ref, y_ref, gp_ref, gn_ref, xo_ref, h_ref):
    xn = x_ref[...] + _rms_f32(y_ref[...].astype(F32), gp_ref[...])
    xo_ref[...] = xn
    h_ref[...] = _rms_f32(xn, gn_ref[...]).astype(h_ref.dtype)


def _resnorm_last_kernel(x_ref, y_ref, gp_ref, xo_ref):
    xo_ref[...] = x_ref[...] + _rms_f32(y_ref[...].astype(F32), gp_ref[...])


def resnorm(x, y, g_post, g_next):
    m, d = x.shape
    bm = _blk(m, 256)
    row = pl.BlockSpec((bm, d), lambda i: (i, 0))
    vec = pl.BlockSpec((1, d), lambda i: (0, 0))
    if g_next is None:
        return pl.pallas_call(
            _resnorm_last_kernel, grid=(m // bm,), in_specs=[row, row, vec], out_specs=row,
            out_shape=jax.ShapeDtypeStruct((m, d), F32), compiler_params=_cparams("parallel"),
            name="resnorm_last",
        )(x, y, g_post.reshape(1, d)), None
    return pl.pallas_call(
        _resnorm_kernel, grid=(m // bm,), in_specs=[row, row, vec, vec], out_specs=[row, row],
        out_shape=[jax.ShapeDtypeStruct((m, d), F32), jax.ShapeDtypeStruct((m, d), BF16)],
        compiler_params=_cparams("parallel"), name="resnorm",
    )(x, y, g_post.reshape(1, d), g_next.reshape(1, d))


def _mm_kernel(a_ref, w_ref, o_ref, *, scale, scale_blocks):
    acc = jnp.dot(a_ref[...], w_ref[...], preferred_element_type=F32)
    if scale is not None:
        sc = scale if scale_blocks is None else jnp.where(pl.program_id(1) < scale_blocks, scale, 1.0)
        acc = acc * sc
    o_ref[...] = acc.astype(o_ref.dtype)


def matmul(a, w, out_dtype, *, n_off=0, n=None, scale=None, scale_cols=None, bm_pref=1024, bn_pref=1024):
    m, k = a.shape
    n = w.shape[1] - n_off if n is None else n
    bm = _blk(m, bm_pref)
    bn = _blk(n, bn_pref)
    while n_off % bn:
        bn //= 2
    assert bn % LANE == 0 or bn == w.shape[1]
    joff = n_off // bn
    scale_blocks = None
    if scale_cols is not None:
        assert scale_cols % bn == 0
        scale_blocks = scale_cols // bn
    return pl.pallas_call(
        functools.partial(_mm_kernel, scale=scale, scale_blocks=scale_blocks),
        grid=(m // bm, n // bn),
        in_specs=[pl.BlockSpec((bm, k), lambda i, j: (i, 0)),
                  pl.BlockSpec((k, bn), lambda i, j: (0, j + joff))],
        out_specs=pl.BlockSpec((bm, bn), lambda i, j: (i, j)),
        out_shape=jax.ShapeDtypeStruct((m, n), out_dtype),
        compiler_params=_cparams("parallel", "parallel"),
        name="matmul",
    )(a, w)


def _mm2_kernel(a1_ref, a2_ref, w1_ref, w2_ref, o_ref):
    acc = jnp.dot(a1_ref[...], w1_ref[...], preferred_element_type=F32)
    acc = acc + jnp.dot(a2_ref[...], w2_ref[...], preferred_element_type=F32)
    o_ref[...] = acc.astype(o_ref.dtype)


def matmul_cat2(a1, a2, w, out_dtype):
    m, k1 = a1.shape
    assert a2.shape == (m, k1) and w.shape[0] == 2 * k1
    n = w.shape[1]
    bm = _blk(m, 1024)
    bn = _blk(n, 1024)
    return pl.pallas_call(
        _mm2_kernel,
        grid=(m // bm, n // bn),
        in_specs=[pl.BlockSpec((bm, k1), lambda i, j: (i, 0)),
                  pl.BlockSpec((bm, k1), lambda i, j: (i, 0)),
                  pl.BlockSpec((k1, bn), lambda i, j: (0, j)),
                  pl.BlockSpec((k1, bn), lambda i, j: (1, j))],
        out_specs=pl.BlockSpec((bm, bn), lambda i, j: (i, j)),
        out_shape=jax.ShapeDtypeStruct((m, n), out_dtype),
        compiler_params=_cparams("parallel", "parallel"),
        name="matmul_cat2",
    )(a1, a2, w, w)


def _mm_acc_kernel(a_ref, w_ref, o_ref, acc_ref, *, nk, bk, k_valid):
    kk = pl.program_id(2)
    w = w_ref[...]
    if k_valid < nk * bk:
        row = lax.broadcasted_iota(jnp.int32, w.shape, 0)
        w = jnp.where(row < k_valid - kk * bk, w, jnp.zeros_like(w))
    part = jnp.dot(a_ref[...], w, preferred_element_type=F32)

    @pl.when(kk == 0)
    def _():
        acc_ref[...] = part

    @pl.when(kk > 0)
    def _():
        acc_ref[...] += part

    @pl.when(kk == nk - 1)
    def _():
        o_ref[...] = acc_ref[...].astype(o_ref.dtype)


def matmul_ktiled(a, w, out_dtype, *, bk):
    m, k = a.shape
    k_valid, n = w.shape
    bm = _blk(m, 1024)
    bn = _blk(n, 1024)
    assert k % bk == 0 and k - bk < k_valid <= k
    nk = k // bk
    return pl.pallas_call(
        functools.partial(_mm_acc_kernel, nk=nk, bk=bk, k_valid=k_valid),
        grid=(m // bm, n // bn, nk),
        in_specs=[pl.BlockSpec((bm, bk), lambda i, j, kk: (i, kk)),
                  pl.BlockSpec((bk, bn), lambda i, j, kk: (kk, j))],
        out_specs=pl.BlockSpec((bm, bn), lambda i, j, kk: (i, j)),
        out_shape=jax.ShapeDtypeStruct((m, n), out_dtype),
        scratch_shapes=[pltpu.VMEM((bm, bn), F32)],
        compiler_params=_cparams("parallel", "parallel", "arbitrary"),
        name="matmul_ktiled",
    )(a, w)


def _swiglu_kernel(a_ref, wg_ref, wu_ref, o_ref, *, bn, ff):
    a = a_ref[...]
    g = jnp.dot(a, wg_ref[...], preferred_element_type=F32)
    u = jnp.dot(a, wu_ref[...], preferred_element_type=F32)
    act = g * jax.nn.sigmoid(g) * u
    col = pl.program_id(1) * bn + lax.broadcasted_iota(jnp.int32, (1, bn), 1)
    o_ref[...] = jnp.where(col < ff, act, 0.0).astype(o_ref.dtype)


def swiglu_up(h, w_gate, w_up, *, ffp):
    m, k = h.shape
    ff = w_gate.shape[1]
    bm = _blk(m, 1024)
    bn = FF_BLOCK
    assert ffp % bn == 0 and ffp - bn < ff <= ffp
    return pl.pallas_call(
        functools.partial(_swiglu_kernel, bn=bn, ff=ff),
        grid=(m // bm, ffp // bn),
        in_specs=[pl.BlockSpec((bm, k), lambda i, j: (i, 0)),
                  pl.BlockSpec((k, bn), lambda i, j: (0, j)),
                  pl.BlockSpec((k, bn), lambda i, j: (0, j))],
        out_specs=pl.BlockSpec((bm, bn), lambda i, j: (i, j)),
        out_shape=jax.ShapeDtypeStruct((m, ffp), BF16),
        compiler_params=_cparams("parallel", "parallel"),
        name="swiglu_up",
    )(h, w_gate, w_up)


def _conv_kernel(a_ref, b_ref, ah_ref, bh_ref, hist_ref, cw_ref, cb_ref, lg_ref, lb_ref, o_ref,
                 buf_ref, y_ref, *, tb, width, halo, tt):
    t_blk = pl.program_id(1)
    a = a_ref[...].astype(F32)
    u_cur = a * jax.nn.sigmoid(b_ref[...].astype(F32))
    @pl.when(t_blk == 0)
    def _():
        buf_ref[0:halo, :] = hist_ref[...]

    @pl.when(t_blk > 0)
    def _():
        ah = ah_ref[...].astype(F32)
        buf_ref[0:halo, :] = ah * jax.nn.sigmoid(bh_ref[...].astype(F32))

    buf_ref[halo:halo + tb, :] = u_cur
    ch = a.shape[1]
    base = halo - (width - 1)

    sub = SUBLANES

    def lane_tile(ct, carry):
        c0 = pl.multiple_of(ct * LANE, LANE)
        bias = cb_ref[:, pl.ds(c0, LANE)]
        for t0 in range(0, tb, tt):
            acc = jnp.broadcast_to(bias, (tt, LANE)).reshape(tt // sub, sub, LANE)
            for ph in range(sub):
                taps = [w for w in range(width) if (base + w) % sub == ph]
                if not taps:
                    continue
                span = tt + taps[-1] - taps[0]
                blk = buf_ref[pl.ds(base + t0 + taps[0], span), pl.ds(c0, LANE)].reshape(span // sub, sub, LANE)
                for w in taps:
                    off = (w - taps[0]) // sub
                    acc = acc + cw_ref[pl.ds(w * sub, sub), pl.ds(c0, LANE)][None] * blk[off:off + tt // sub]
            y_ref[pl.ds(t0, tt), pl.ds(c0, LANE)] = acc.reshape(tt, LANE)
        return carry

    lax.fori_loop(0, ch // LANE, lane_tile, 0)
    y = y_ref[...]
    mu = jnp.mean(y, axis=-1, keepdims=True)
    yc = y - mu
    var = jnp.mean(yc * yc, axis=-1, keepdims=True)
    z = yc * lax.rsqrt(var + EPS) * lg_ref[...] + lb_ref[...]
    o_ref[...] = (z * jax.nn.sigmoid(z)).astype(o_ref.dtype)


def conv_module(z_main, hist, conv_w, conv_b, ln_g, ln_b, *, nb, t, ch):
    width = conv_w.shape[0]
    halo = hist.shape[1]
    tb = _blk(t, 256)
    tt = _blk(tb, 64)
    nt = t // tb
    hb = tb // halo
    cw = jnp.repeat(conv_w, SUBLANES, axis=0)
    row = lambda c: pl.BlockSpec((tb, ch), lambda b, i: (b * nt + i, c))
    prev = lambda c: pl.BlockSpec((halo, ch), lambda b, i: (jnp.maximum((b * nt + i) * hb - 1, 0), c))
    vec = pl.BlockSpec((1, ch), lambda b, i: (0, 0))
    return pl.pallas_call(
        functools.partial(_conv_kernel, tb=tb, width=width, halo=halo, tt=tt),
        grid=(nb, nt),
        in_specs=[row(0), row(1), prev(0), prev(1),
                  pl.BlockSpec((None, halo, ch), lambda b, i: (b, 0, 0)),
                  pl.BlockSpec((width * SUBLANES, ch), lambda b, i: (0, 0)), vec, vec, vec],
        out_specs=pl.BlockSpec((tb, ch), lambda b, i: (b * nt + i, 0)),
        out_shape=jax.ShapeDtypeStruct((nb * t, ch), BF16),
        scratch_shapes=[pltpu.VMEM((halo + tb, ch), F32), pltpu.VMEM((tb, ch), F32)],
        compiler_params=_cparams("parallel", "parallel"),
        name="conv_module",
    )(z_main, z_main, z_main, z_main, hist, cw, conv_b.reshape(1, ch), ln_g.reshape(1, ch),
      ln_b.reshape(1, ch))


def _glu_tail_kernel(z_ref, o_ref, *, ch, keep):
    z = z_ref[...]
    u = z[:, :ch] * jax.nn.sigmoid(z[:, ch:])
    o_ref[...] = u[u.shape[0] - keep:, :]


def glu_tail(z_tail, *, nb, rows, ch, keep):
    return pl.pallas_call(
        functools.partial(_glu_tail_kernel, ch=ch, keep=keep),
        grid=(nb,),
        in_specs=[pl.BlockSpec((rows, 2 * ch), lambda b: (b, 0))],
        out_specs=pl.BlockSpec((None, keep, ch), lambda b: (b, 0, 0)),
        out_shape=jax.ShapeDtypeStruct((nb, keep, ch), F32),
        compiler_params=_cparams("parallel"),
        name="glu_tail",
    )(z_tail)


def _rope_pairs(v, cc, ss):
    return v * cc + pltpu.roll(v, LANE // 2, axis=1) * ss


def _mla_kv_prep_kernel(z_ref, g_ref, cc_ref, ss_ref, lat_ref, kpe_ref, a_ref, *, kvl):
    z = z_ref[...]
    lat = _rms_f32(z[:, :kvl], g_ref[...])
    rot = _rope_pairs(z[:, kvl:kvl + LANE], cc_ref[...], ss_ref[...])
    lat_ref[...] = lat
    kpe_ref[...] = rot[:, :LANE // 2]
    a_ref[:, :kvl] = lat.astype(a_ref.dtype)
    a_ref[:, kvl:] = rot.astype(a_ref.dtype)


def mla_kv_prep(z_kv, g, cc, ss, *, t):
    m, wdt = z_kv.shape
    kvl = wdt - LANE
    bm = _blk(t, 256)
    nt = t // bm
    row = lambda wd: pl.BlockSpec((bm, wd), lambda i: (i, 0))
    tab = pl.BlockSpec((bm, LANE), lambda i: (i % nt, 0))
    return pl.pallas_call(
        functools.partial(_mla_kv_prep_kernel, kvl=kvl),
        grid=(m // bm,),
        in_specs=[row(wdt), pl.BlockSpec((1, kvl), lambda i: (0, 0)), tab, tab],
        out_specs=[row(kvl), row(LANE // 2), row(wdt)],
        out_shape=[jax.ShapeDtypeStruct((m, kvl), F32), jax.ShapeDtypeStruct((m, LANE // 2), F32),
                   jax.ShapeDtypeStruct((m, wdt), BF16)],
        compiler_params=_cparams("parallel"),
        name="mla_kv_prep",
    )(z_kv, g.reshape(1, kvl), cc, ss)


def _mla_q_kernel(z_ref, g_ref, w_ref, cc_ref, ss_ref, o_ref, *, heads, scale):
    a = _rms_f32(z_ref[...].astype(F32), g_ref[...]).astype(BF16)
    acc = jnp.dot(a, w_ref[...], preferred_element_type=F32)
    cc = cc_ref[...]
    ss = ss_ref[...]
    for h in range(heads):
        c0 = h * MXU_DIM
        o_ref[:, c0:c0 + LANE] = (acc[:, c0:c0 + LANE] * scale).astype(o_ref.dtype)
        rot = _rope_pairs(acc[:, c0 + LANE:c0 + MXU_DIM], cc, ss)
        o_ref[:, c0 + LANE:c0 + MXU_DIM] = (rot * scale).astype(o_ref.dtype)


def mla_q_proj(z_main, col_blk, g, w_uq_ext, cc, ss, *, t, heads, scale):
    m = z_main.shape[0]
    ql, n = w_uq_ext.shape
    bm = _blk(t, 512)
    nt = t // bm
    tab = pl.BlockSpec((bm, LANE), lambda i: (i % nt, 0))
    return pl.pallas_call(
        functools.partial(_mla_q_kernel, heads=heads, scale=scale),
        grid=(m // bm,),
        in_specs=[pl.BlockSpec((bm, ql), lambda i: (i, col_blk)),
                  pl.BlockSpec((1, ql), lambda i: (0, 0)),
                  pl.BlockSpec((ql, n), lambda i: (0, 0)), tab, tab],
        out_specs=pl.BlockSpec((bm, n), lambda i: (i, 0)),
        out_shape=jax.ShapeDtypeStruct((m, n), BF16),
        compiler_params=_cparams("parallel"),
        name="mla_q_proj",
    )(z_main, g.reshape(1, ql), w_uq_ext, cc, ss)


def _mla_attn_kernel(q_ref, k_ref, v_ref, o_ref, m_ref, l_ref, acc_ref, *, heads, dv, bq, bk, nk,
                     mode, kv_len):
    i = pl.program_id(1)
    j = pl.program_id(2)

    @pl.when(j == 0)
    def _():
        m_ref[...] = jnp.full(m_ref.shape, NEG, F32)
        l_ref[...] = jnp.zeros(l_ref.shape, F32)
        acc_ref[...] = jnp.zeros(acc_ref.shape, F32)

    def step(mask):
        for h in range(heads):
            qh = q_ref[:, h * MXU_DIM:(h + 1) * MXU_DIM]
            kh = k_ref[:, h * MXU_DIM:(h + 1) * MXU_DIM]
            s = lax.dot_general(qh, kh, (((1,), (1,)), ((), ())), preferred_element_type=F32)
            if mask is not None:
                s = jnp.where(mask, s, NEG)
            tiles = [s[:, t * LANE:(t + 1) * LANE] for t in range(bk // LANE)]
            mx = tiles[0]
            for tl in tiles[1:]:
                mx = jnp.maximum(mx, tl)
            m_prev = m_ref[h]
            m_new = jnp.maximum(m_prev, jnp.max(mx, axis=1, keepdims=True))
            alpha = jnp.exp2(m_prev - m_new)
            ps = [jnp.exp2(tl - m_new) for tl in tiles]
            lsum = ps[0]
            for p in ps[1:]:
                lsum = lsum + p
            l_ref[h] = alpha * l_ref[h] + lsum
            pb = jnp.concatenate(ps, axis=1).astype(BF16)
            pv = jnp.dot(pb, v_ref[:, h * dv:(h + 1) * dv], preferred_element_type=F32)
            acc_ref[:, h * dv:(h + 1) * dv] = alpha * acc_ref[:, h * dv:(h + 1) * dv] + pv
            m_ref[h] = m_new

    if mode == "block_causal":
        @pl.when(j < i)
        def _():
            step(None)

        @pl.when(j == i)
        def _():
            qc = lax.broadcasted_iota(jnp.int32, (bq, bk), 0) // CHUNK
            kc = lax.broadcasted_iota(jnp.int32, (bq, bk), 1) // CHUNK
            step(kc <= qc)
    else:
        @pl.when(j < nk - 1)
        def _():
            step(None)

        @pl.when(j == nk - 1)
        def _():
            kpos = (nk - 1) * bk + lax.broadcasted_iota(jnp.int32, (bq, bk), 1)
            step(kpos < kv_len if kv_len < nk * bk else None)

    @pl.when(j == nk - 1)
    def _():
        for h in range(heads):
            l = jnp.sum(l_ref[h], axis=1, keepdims=True)
            o_ref[:, h * dv:(h + 1) * dv] = (acc_ref[:, h * dv:(h + 1) * dv] / l).astype(o_ref.dtype)


def mla_attention(q, kv, *, nb, tq, lk, kv_len, heads, dv, mode):
    kw = heads * MXU_DIM
    vw = heads * dv
    assert kw % vw == 0 and dv == LANE
    bq = _blk(tq, 512)
    if mode == "block_causal":
        assert lk == tq and kv_len == lk and bq % CHUNK == 0
        bk = bq
    else:
        bk = _blk(lk, 768)
    nq, nk = tq // bq, lk // bk
    assert kv_len > (nk - 1) * bk, "key padding must sit in the last key block"
    if mode == "block_causal":
        kidx = lambda b, i, j: (b * nk + jnp.minimum(j, i), 0)
        vidx = lambda b, i, j: (b * nk + jnp.minimum(j, i), kw // vw)
    else:
        kidx = lambda b, i, j: (b * nk + j, 0)
        vidx = lambda b, i, j: (b * nk + j, kw // vw)
    return pl.pallas_call(
        functools.partial(_mla_attn_kernel, heads=heads, dv=dv, bq=bq, bk=bk, nk=nk, mode=mode,
                          kv_len=kv_len),
        grid=(nb, nq, nk),
        in_specs=[pl.BlockSpec((bq, kw), lambda b, i, j: (b * nq + i, 0)),
                  pl.BlockSpec((bk, kw), kidx),
                  pl.BlockSpec((bk, vw), vidx)],
        out_specs=pl.BlockSpec((bq, vw), lambda b, i, j: (b * nq + i, 0)),
        out_shape=jax.ShapeDtypeStruct((nb * tq, vw), BF16),
        scratch_shapes=[pltpu.VMEM((heads, bq, LANE), F32), pltpu.VMEM((heads, bq, LANE), F32),
                        pltpu.VMEM((bq, vw), F32)],
        compiler_params=_cparams("parallel", "parallel", "arbitrary"),
        name="mla_attention",
    )(q, kv, kv)


def _band_bias_kernel(t_ref, o_ref, *, rows, max_rel, kb, past):
    t = t_ref[...]
    hi = t.astype(BF16)
    r1 = t - hi.astype(F32)
    mid = r1.astype(BF16)
    lo = (r1 - mid.astype(F32)).astype(BF16)
    nrel = t.shape[1]
    j = lax.broadcasted_iota(jnp.int32, (nrel, kb), 1)
    r = lax.broadcasted_iota(jnp.int32, (nrel, kb), 0)
    kc = lax.broadcasted_iota(jnp.int32, (t.shape[0], kb), 1) // CHUNK - past // CHUNK
    for rr in range(rows):
        i = pl.program_id(0) * rows + rr
        idx = jnp.clip(i - j + past, -max_rel, max_rel) + max_rel
        onehot = jnp.where(r == idx, 1.0, 0.0).astype(BF16)
        vals = (jnp.dot(hi, onehot, preferred_element_type=F32)
                + jnp.dot(mid, onehot, preferred_element_type=F32)
                + jnp.dot(lo, onehot, preferred_element_type=F32))
        qc = i // CHUNK
        ok = (kc <= qc) & (kc >= qc - BAND_CHUNKS)
        o_ref[rr] = jnp.where(ok, vals * LOG2E, NEG)


def band_bias(table):
    heads, nrel = table.shape
    max_rel = (nrel - 1) // 2
    nrel_p = -(-nrel // LANE) * LANE
    tp = jnp.pad(table, ((0, 0), (0, nrel_p - nrel)))
    rows = 8
    past = BAND_CHUNKS * CHUNK
    return pl.pallas_call(
        functools.partial(_band_bias_kernel, rows=rows, max_rel=max_rel, kb=BAND_KB, past=past),
        grid=(BAND_QB // rows,),
        in_specs=[pl.BlockSpec((heads, nrel_p), lambda i: (0, 0))],
        out_specs=pl.BlockSpec((rows, heads, BAND_KB), lambda i: (i, 0, 0)),
        out_shape=jax.ShapeDtypeStruct((BAND_QB, heads, BAND_KB), F32),
        compiler_params=_cparams("parallel"),
        name="band_bias",
    )(tp)


def _softmax_pv(tiles, v_parts):
    mx = tiles[0]
    for tl in tiles[1:]:
        mx = jnp.maximum(mx, tl)
    m = jnp.max(mx, axis=1, keepdims=True)
    ps = [jnp.exp2(tl - m) for tl in tiles]
    lsum = ps[0]
    for p in ps[1:]:
        lsum = lsum + p
    l = jnp.sum(lsum, axis=1, keepdims=True)
    per = len(tiles) // len(v_parts)
    o = None
    for n, vp in enumerate(v_parts):
        pb = jnp.concatenate(ps[n * per:(n + 1) * per], axis=1).astype(BF16)
        part = jnp.dot(pb, vp, preferred_element_type=F32)
        o = part if o is None else o + part
    return o / l


def _lane_tiles(s):
    return [s[:, c:c + LANE] for c in range(0, s.shape[1], LANE)]


def _band_prompt_kernel(q_ref, k0_ref, k1_ref, k2_ref, v0_ref, v1_ref, v2_ref, bias_ref, o_ref, *,
                        group, dh):
    qb = pl.program_id(2)
    pen = [jnp.where(qb < 2, NEG, 0.0), jnp.where(qb < 1, NEG, 0.0), None]
    for g in range(group):
        sl = slice(g * dh, (g + 1) * dh)
        qh = q_ref[:, sl]
        tiles = []
        for n, kr in enumerate((k0_ref, k1_ref, k2_ref)):
            part = lax.dot_general(qh, kr[:, sl], (((1,), (1,)), ((), ())), preferred_element_type=F32)
            for c in range(0, BAND_QB, LANE):
                tl = part[:, c:c + LANE] + bias_ref[g, :, n * BAND_QB + c:n * BAND_QB + c + LANE]
                tiles.append(tl if pen[n] is None else tl + pen[n])
        o = _softmax_pv(tiles, [v0_ref[:, sl], v1_ref[:, sl], v2_ref[:, sl]])
        o_ref[:, sl] = o.astype(o_ref.dtype)


def band_attention_prompt(qkv, bias, *, nb, t, heads, dh):
    d = heads * dh
    group = min(heads, 8)
    ng = heads // group
    gw = group * dh
    nq = t // BAND_QB
    assert t % BAND_QB == 0 and BAND_KB == 3 * BAND_QB

    def kspec(back, col0):
        return pl.BlockSpec((BAND_QB, gw),
                            lambda hg, b, i: (b * nq + jnp.maximum(i - back, 0), col0 + hg))

    return pl.pallas_call(
        functools.partial(_band_prompt_kernel, group=group, dh=dh),
        grid=(ng, nb, nq),
        in_specs=[pl.BlockSpec((BAND_QB, gw), lambda hg, b, i: (b * nq + i, hg)),
                  kspec(2, ng), kspec(1, ng), kspec(0, ng),
                  kspec(2, 2 * ng), kspec(1, 2 * ng), kspec(0, 2 * ng),
                  pl.BlockSpec((group, BAND_QB, BAND_KB), lambda hg, b, i: (hg, 0, 0))],
        out_specs=pl.BlockSpec((BAND_QB, gw), lambda hg, b, i: (b * nq + i, hg)),
        out_shape=jax.ShapeDtypeStruct((nb * t, d), BF16),
        compiler_params=_cparams("parallel", "parallel", "parallel"),
        name="band_attention_prompt",
    )(qkv, qkv, qkv, qkv, qkv, qkv, qkv, bias)


def _band_step_kernel(q_ref, k_ref, v_ref, bias_ref, o_ref, *, group, dh):
    for g in range(group):
        sl = slice(g * dh, (g + 1) * dh)
        s = lax.dot_general(q_ref[:, sl], k_ref[:, sl], (((1,), (1,)), ((), ())),
                            preferred_element_type=F32) + bias_ref[g]
        o = _softmax_pv(_lane_tiles(s), [v_ref[:, sl]])
        o_ref[:, sl] = o.astype(o_ref.dtype)


def band_attention_step(qkv, k_all, v_all, bias, *, nb, t, heads, dh):
    d = heads * dh
    group = min(heads, 8)
    ng = heads // group
    gw = group * dh
    lk = k_all.shape[0] // nb
    return pl.pallas_call(
        functools.partial(_band_step_kernel, group=group, dh=dh),
        grid=(ng, nb),
        in_specs=[pl.BlockSpec((t, gw), lambda hg, b: (b, hg)),
                  pl.BlockSpec((lk, gw), lambda hg, b: (b, hg)),
                  pl.BlockSpec((lk, gw), lambda hg, b: (b, hg)),
                  pl.BlockSpec((group, t, lk), lambda hg, b: (hg, 0, 0))],
        out_specs=pl.BlockSpec((t, gw), lambda hg, b: (b, hg)),
        out_shape=jax.ShapeDtypeStruct((nb * t, d), BF16),
        compiler_params=_cparams("parallel", "parallel"),
        name="band_attention_step",
    )(qkv, k_all, v_all, bias)


def _mem_attn_kernel(q_ref, k_ref, v_ref, o_ref, *, heads, dh):
    for h in range(heads):
        sl = slice(h * dh, (h + 1) * dh)
        s = lax.dot_general(q_ref[:, sl], k_ref[:, sl], (((1,), (1,)), ((), ())),
                            preferred_element_type=F32)
        o = _softmax_pv(_lane_tiles(s), [v_ref[:, sl]])
        o_ref[:, sl] = o.astype(o_ref.dtype)


def mem_attention(q, mk, mv, *, nb, t, heads, dh):
    d = heads * dh
    ml = mk.shape[0] // nb
    bq = _blk(t, 512)
    nq = t // bq
    return pl.pallas_call(
        functools.partial(_mem_attn_kernel, heads=heads, dh=dh),
        grid=(nb, nq),
        in_specs=[pl.BlockSpec((bq, d), lambda b, i: (b * nq + i, 0)),
                  pl.BlockSpec((ml, d), lambda b, i: (b, 0)),
                  pl.BlockSpec((ml, d), lambda b, i: (b, 0))],
        out_specs=pl.BlockSpec((bq, d), lambda b, i: (b * nq + i, 0)),
        out_shape=jax.ShapeDtypeStruct((nb * t, d), BF16),
        compiler_params=_cparams("parallel", "parallel"),
        name="mem_attention",
    )(q, mk, mv)


def _cast_kernel(x_ref, o_ref):
    o_ref[...] = x_ref[...].astype(o_ref.dtype)


def cast_bf16(w, *, col_blk=0, cols=None):
    nl, r, c = w.shape
    cols = c if cols is None else cols
    assert cols == c or cols % LANE == 0
    br = _blk(r, 1 << max(3, (CAST_BLOCK_BYTES // (4 * cols)).bit_length() - 1))
    return pl.pallas_call(
        _cast_kernel,
        grid=(nl, r // br),
        in_specs=[pl.BlockSpec((None, br, cols), lambda l, i: (l, i, col_blk))],
        out_specs=pl.BlockSpec((None, br, cols), lambda l, i: (l, i, 0)),
        out_shape=jax.ShapeDtypeStruct((nl, r, cols), BF16),
        compiler_params=_cparams("parallel", "parallel"),
        name="cast_bf16",
    )(w)


def _prep_weights(W, dims):
    heads, nope, rope, dv, kvl = dims["mla_heads"], dims["nope"], dims["rope"], dims["dv"], dims["kvl"]
    ch, ql = dims["ch"], dims["ql"]
    half = rope // 2
    P = {}
    w_in = W["w_in_ab"]
    s2 = 2 * ch + ql
    P["w_in_main"] = cast_bf16(w_in, col_blk=0, cols=s2)
    kpe0 = s2 + kvl
    P["w_in_kv"] = jnp.concatenate(
        [w_in[:, :, s2:], w_in[:, :, kpe0 + half:kpe0 + rope], w_in[:, :, kpe0:kpe0 + half]],
        axis=-1).astype(BF16)
    ne = w_in.shape[0]
    wq = W["w_uq"].reshape(ne, ql, heads, nope + rope)
    pe = wq[..., nope:]
    pe_sw = jnp.concatenate([pe[..., half:], pe[..., :half]], axis=-1)
    P["w_uq"] = jnp.concatenate([wq[..., :nope], pe, pe_sw], axis=-1).reshape(
        ne, ql, heads * MXU_DIM).astype(BF16)
    wkv = W["w_ukv"].reshape(ne, kvl, heads, nope + dv)
    kslot = jnp.pad(wkv[..., :nope], ((0, 0), (0, LANE), (0, 0), (0, MXU_DIM - nope)))
    eye = jnp.zeros((LANE, MXU_DIM), F32).at[jnp.arange(rope), nope + jnp.arange(rope)].set(1.0)
    kslot = kslot + jnp.pad(eye, ((kvl, 0), (0, 0)))[None, :, None, :]
    vpart = jnp.pad(wkv[..., nope:], ((0, 0), (0, LANE), (0, 0), (0, 0)))
    P["w_ukv"] = jnp.concatenate(
        [kslot.reshape(ne, kvl + LANE, heads * MXU_DIM), vpart.reshape(ne, kvl + LANE, heads * dv)],
        axis=-1).astype(BF16)
    for name in ("w_out_ab", "w_qkv_c", "w_out_c", "w_q_mem", "w_kv_mem", "w_o_mem", "w_down"):
        P[name] = cast_bf16(W[name])
    ff = W["w_down"].shape[1]
    P["w_gate"] = cast_bf16(W["w_gate_up"], col_blk=0, cols=ff)
    P["w_up"] = cast_bf16(W["w_gate_up"], col_blk=1, cols=ff)
    P["ffp"] = -(-ff // FF_BLOCK) * FF_BLOCK
    return P


def _rope_tabs(pos, rope):
    half = rope // 2
    inv = ROPE_THETA ** (-jnp.arange(0, rope, 2, dtype=F32) / rope)
    ang = pos.astype(F32)[:, None] * inv[None, :]
    cos, sin = jnp.cos(ang), jnp.sin(ang)
    zero = jnp.zeros((pos.shape[0], LANE - rope), F32)
    assert 2 * rope == LANE and half * 2 == rope
    return jnp.concatenate([cos, cos, zero], axis=1), jnp.concatenate([-sin, sin, zero], axis=1)


def _trunk(x3, pos0, mem, mem_k_c, mem_v_c, lat_c, kpe_c, conv_c, bk_c, bv_c, W, P, dims, bias_full):
    prompt = mem is not None
    nb, t, d = x3.shape
    m = nb * t
    depth = W["norm_mix_pre"].shape[0]
    ch, ql, kvl, rope = dims["ch"], dims["ql"], dims["kvl"], dims["rope"]
    mh, nope, dv = dims["mla_heads"], dims["nope"], dims["dv"]
    cheads, cdh = dims["c_heads"], dims["c_dh"]
    memh, memdh = dims["mem_heads"], dims["mem_dh"]
    width = W["conv_w"].shape[1]
    keep = width - 1
    halo = 32
    assert keep <= halo and t % halo == 0 and (2 * ch) % ql == 0
    band_keep = BAND_CHUNKS * CHUNK

    x = x3.reshape(m, d)
    cc, ss = _rope_tabs(pos0 + jnp.arange(t, dtype=jnp.int32), rope)
    q_scale = (nope + rope) ** -0.5 * LOG2E
    lats, kpes, convs, bks, bvs, mks, mvs = [], [], [], [], [], [], []

    def tail_rows(h, rows):
        return h.reshape(nb, t, d)[:, t - rows:].reshape(nb * rows, d)

    h = rmsnorm(x, W["norm_mix_pre"][0])
    for l in range(depth):
        if l % 2 == 0:
            e = l // 2
            z_main = matmul(h, P["w_in_main"][e], BF16)
            z_kv = matmul(h, P["w_in_kv"][e], F32)
            if prompt:
                hist = jnp.zeros((nb, halo, ch), F32)
            else:
                hist = jnp.pad(conv_c[e], ((0, 0), (halo - keep, 0), (0, 0)))
            ya = conv_module(z_main, hist, W["conv_w"][e], W["conv_b"][e], W["conv_ln_g"][e],
                             W["conv_ln_b"][e], nb=nb, t=t, ch=ch)
            z_tail = matmul(tail_rows(h, halo), P["w_in_main"][e], F32, n=2 * ch)
            convs.append(glu_tail(z_tail, nb=nb, rows=halo, ch=ch, keep=keep))
            lat, kpe, a_kv = mla_kv_prep(z_kv, W["kv_norm_g"][e], cc, ss, t=t)
            lats.append(lat.reshape(nb, t, kvl))
            kpes.append(kpe.reshape(nb, t, rope))
            q = mla_q_proj(z_main, (2 * ch) // ql, W["q_norm_g"][e], P["w_uq"][e], cc, ss, t=t,
                           heads=mh, scale=q_scale)
            if prompt:
                lk, kv_len, mode = t, t, "block_causal"
                a_all = a_kv
            else:
                past = lat_c.shape[2]
                kv_len = past + t
                assert (kv_len - 1) // CHUNK <= (past // CHUNK), "decode keys must all be visible"
                lk = -(-kv_len // (3 * LANE)) * (3 * LANE)
                a_hist = jnp.concatenate(
                    [lat_c[e], kpe_c[e], jnp.zeros((nb, past, LANE - rope), F32)], axis=-1).astype(BF16)
                a_all = jnp.concatenate(
                    [a_hist, a_kv.reshape(nb, t, kvl + LANE),
                     jnp.zeros((nb, lk - kv_len, kvl + LANE), BF16)], axis=1).reshape(nb * lk, kvl + LANE)
                mode = "all_visible"
            kv = matmul(a_all, P["w_ukv"][e], BF16)
            yb = mla_attention(q, kv, nb=nb, tq=t, lk=lk, kv_len=kv_len, heads=mh, dv=dv, mode=mode)
            y = matmul_cat2(ya, yb, P["w_out_ab"][e], Y_DTYPE)
        else:
            o = l // 2
            dc = cheads * cdh
            qkv = matmul(h, P["w_qkv_c"][o], BF16, scale=cdh ** -0.5 * LOG2E, scale_cols=dc)
            if prompt:
                yc = band_attention_prompt(qkv, bias_full[o], nb=nb, t=t, heads=cheads, dh=cdh)
                rows = min(band_keep, t)
            else:
                k_new = qkv[:, dc:2 * dc].reshape(nb, t, dc)
                v_new = qkv[:, 2 * dc:].reshape(nb, t, dc)
                lh = bk_c.shape[2]
                assert lh == band_keep and t == CHUNK and pos0 % CHUNK == 0
                lkp = -(-(lh + t) // LANE) * LANE
                zpad = jnp.zeros((nb, lkp - lh - t, dc), BF16)
                k_all = jnp.concatenate([bk_c[o].reshape(nb, lh, dc).astype(BF16), k_new, zpad], axis=1)
                v_all = jnp.concatenate([bv_c[o].reshape(nb, lh, dc).astype(BF16), v_new, zpad], axis=1)
                bias_s = jnp.pad(bias_full[o][:, :t, :lh + t], ((0, 0), (0, 0), (0, lkp - lh - t)),
                                 constant_values=NEG)
                yc = band_attention_step(qkv, k_all.reshape(nb * lkp, dc), v_all.reshape(nb * lkp, dc),
                                         bias_s, nb=nb, t=t, heads=cheads, dh=cdh)
                rows = t
            kv_tail = matmul(tail_rows(h, rows), P["w_qkv_c"][o], F32, n_off=dc, n=2 * dc)
            bks.append(kv_tail[:, :dc].reshape(nb, rows, cheads, cdh))
            bvs.append(kv_tail[:, dc:].reshape(nb, rows, cheads, cdh))
            y = matmul(yc, P["w_out_c"][o], Y_DTYPE)
        x, h = resnorm(x, y, W["norm_mix_post"][l], W["norm_mem_pre"][l])

        md = memh * memdh
        if prompt:
            ml = mem.shape[1]
            hm = rmsnorm(mem.reshape(nb * ml, d), W["norm_memkv"][l])
            mkv = matmul(hm, P["w_kv_mem"][l], F32)
            mk32, mv32 = mkv[:, :md], mkv[:, md:]
            mks.append(mk32.reshape(nb, ml, memh, memdh))
            mvs.append(mv32.reshape(nb, ml, memh, memdh))
        else:
            ml = mem_k_c.shape[2]
            mk32 = mem_k_c[l].reshape(nb * ml, md)
            mv32 = mem_v_c[l].reshape(nb * ml, md)
        qm = matmul(h, P["w_q_mem"][l], BF16, scale=memdh ** -0.5 * LOG2E)
        om = mem_attention(qm, mk32.astype(BF16), mv32.astype(BF16), nb=nb, t=t, heads=memh, dh=memdh)
        y = matmul(om, P["w_o_mem"][l], Y_DTYPE)
        x, h = resnorm(x, y, W["norm_mem_post"][l], W["norm_ffn_pre"][l])

        act = swiglu_up(h, P["w_gate"][l], P["w_up"][l], ffp=P["ffp"])
        y = matmul_ktiled(act, P["w_down"][l], Y_DTYPE, bk=P["ffp"] // 4)
        g_next = W["norm_mix_pre"][l + 1] if l + 1 < depth else None
        x, h = resnorm(x, y, W["norm_ffn_post"][l], g_next)

    st = lambda xs: jnp.stack(xs, 0)
    return (x.reshape(nb, t, d), st(lats), st(kpes), st(convs), st(bks), st(bvs),
            st(mks) if prompt else None, st(mvs) if prompt else None)


def kernel(x_prompt, x_sample, cache_mla_latent, cache_mla_kpe, state_conv, cache_band_k,
           cache_band_v, cache_mem_k, cache_mem_v, mem_prompt, norm_mix_pre, norm_mix_post,
           norm_mem_pre, norm_mem_post, norm_memkv, norm_ffn_pre, norm_ffn_post, w_in_ab,
           conv_w, conv_b, conv_ln_g, conv_ln_b, q_norm_g, w_uq, kv_norm_g, w_ukv, w_out_ab,
           w_qkv_c, rel_bias_c, w_out_c, w_q_mem, w_kv_mem, w_o_mem, w_gate_up, w_down):
    W = dict(norm_mix_pre=norm_mix_pre, norm_mix_post=norm_mix_post, norm_mem_pre=norm_mem_pre,
             norm_mem_post=norm_mem_post, norm_memkv=norm_memkv, norm_ffn_pre=norm_ffn_pre,
             norm_ffn_post=norm_ffn_post, w_in_ab=w_in_ab, conv_w=conv_w, conv_b=conv_b,
             conv_ln_g=conv_ln_g, conv_ln_b=conv_ln_b, q_norm_g=q_norm_g, w_uq=w_uq,
             kv_norm_g=kv_norm_g, w_ukv=w_ukv, w_out_ab=w_out_ab, w_qkv_c=w_qkv_c,
             rel_bias_c=rel_bias_c, w_out_c=w_out_c, w_q_mem=w_q_mem, w_kv_mem=w_kv_mem,
             w_o_mem=w_o_mem, w_gate_up=w_gate_up, w_down=w_down)
    ch = conv_w.shape[2]
    ql = q_norm_g.shape[1]
    kvl = kv_norm_g.shape[1]
    rope = cache_mla_kpe.shape[3]
    c_heads, c_dh = cache_band_k.shape[3], cache_band_k.shape[4]
    mem_heads, mem_dh = cache_mem_k.shape[3], cache_mem_k.shape[4]
    a_cols, b_cols, v_cols = w_uq.shape[2], w_ukv.shape[2], w_out_ab.shape[1] - ch
    mla_heads = (a_cols - b_cols + v_cols) // rope
    dims = dict(ch=ch, ql=ql, kvl=kvl, rope=rope, mla_heads=mla_heads,
                nope=(b_cols - v_cols) // mla_heads, dv=v_cols // mla_heads,
                c_heads=c_heads, c_dh=c_dh, mem_heads=mem_heads, mem_dh=mem_dh)
    assert dims["nope"] == LANE and dims["dv"] == LANE and rope == LANE // 2

    P = _prep_weights(W, dims)
    bias_full = jnp.stack([jnp.transpose(band_bias(rel_bias_c[o]), (1, 0, 2))
                           for o in range(rel_bias_c.shape[0])], 0)

    past = cache_mla_latent.shape[2]
    y_p, lat_p, kpe_p, conv_p, bk_p, bv_p, mk_p, mv_p = _trunk(
        x_prompt, 0, mem_prompt, None, None, None, None, None, None, None, W, P, dims, bias_full)
    y_s, lat_s, kpe_s, conv_s, bk_s, bv_s, _, _ = _trunk(
        x_sample, past, None, cache_mem_k, cache_mem_v, cache_mla_latent, cache_mla_kpe,
        state_conv, cache_band_k, cache_band_v, W, P, dims, bias_full)
    return (y_p, y_s, lat_p, kpe_p, conv_p, bk_p, bv_p, mk_p, mv_p, lat_s, kpe_s, conv_s, bk_s, bv_s)
```

```python
import functools

import jax
import jax.numpy as jnp
from jax import lax
from jax.experimental import pallas as pl
from jax.experimental.pallas import tpu as pltpu

F32 = jnp.float32
BF16 = jnp.bfloat16
Y_DTYPE = BF16

EPS = 1e-6
NEG = -1e30
CHUNK = 64
BAND_CHUNKS = 8
ROPE_THETA = 10000.0
LOG2E = 1.4426950408889634

LANE = 128
SUBLANES = 8
MXU_DIM = 256
VMEM_LIMIT_BYTES = 56 * 1024 * 1024
FF_BLOCK = 512
CAST_BLOCK_BYTES = 8 * 1024 * 1024

BAND_QB = 4 * CHUNK
BAND_KB = BAND_CHUNKS * CHUNK + BAND_QB


def _cparams(*sem):
    return pltpu.CompilerParams(dimension_semantics=sem, vmem_limit_bytes=VMEM_LIMIT_BYTES)


def _blk(n, pref):
    if n <= pref:
        return n
    b = pref
    while n % b:
        b //= 2
    assert b >= 8, (n, pref)
    return b


def _rms_f32(x, g):
    return x * lax.rsqrt(jnp.mean(x * x, axis=-1, keepdims=True) + EPS) * g


def _rmsnorm_kernel(x_ref, g_ref, o_ref):
    o_ref[...] = _rms_f32(x_ref[...].astype(F32), g_ref[...]).astype(o_ref.dtype)


def rmsnorm(x, g, out_dtype=BF16):
    m, d = x.shape
    bm = _blk(m, 256)
    return pl.pallas_call(
        _rmsnorm_kernel,
        grid=(m // bm,),
        in_specs=[pl.BlockSpec((bm, d), lambda i: (i, 0)), pl.BlockSpec((1, d), lambda i: (0, 0))],
        out_specs=pl.BlockSpec((bm, d), lambda i: (i, 0)),
        out_shape=jax.ShapeDtypeStruct((m, d), out_dtype),
        compiler_params=_cparams("parallel"),
        name="rmsnorm",
    )(x, g.reshape(1, d))


def _resnorm_kernel(x_ref, y_ref, gp_ref, gn_ref, xo_ref, h_ref):
    xn = x_ref[...] + _rms_f32(y_ref[...].astype(F32), gp_ref[...])
    xo_ref[...] = xn
    h_ref[...] = _rms_f32(xn, gn_ref[...]).astype(h_ref.dtype)


def _resnorm_last_kernel(x_ref, y_ref, gp_ref, xo_ref):
    xo_ref[...] = x_ref[...] + _rms_f32(y_ref[...].astype(F32), gp_ref[...])


def resnorm(x, y, g_post, g_next):
    m, d = x.shape
    bm = _blk(m, 256)
    row = pl.BlockSpec((bm, d), lambda i: (i, 0))
    vec = pl.BlockSpec((1, d), lambda i: (0, 0))
    if g_next is None:
        return pl.pallas_call(
            _resnorm_last_kernel, grid=(m // bm,), in_specs=[row, row, vec], out_specs=row,
            out_shape=jax.ShapeDtypeStruct((m, d), F32), compiler_params=_cparams("parallel"),
            name="resnorm_last",
        )(x, y, g_post.reshape(1, d)), None
    return pl.pallas_call(
        _resnorm_kernel, grid=(m // bm,), in_specs=[row, row, vec, vec], out_specs=[row, row],
        out_shape=[jax.ShapeDtypeStruct((m, d), F32), jax.ShapeDtypeStruct((m, d), BF16)],
        compiler_params=_cparams("parallel"), name="resnorm",
    )(x, y, g_post.reshape(1, d), g_next.reshape(1, d))


def _mm_kernel(a_ref, w_ref, o_ref, *, scale, scale_blocks):
    acc = jnp.dot(a_ref[...], w_ref[...], preferred_element_type=F32)
    if scale is not None:
        sc = scale if scale_blocks is None else jnp.where(pl.program_id(1) < scale_blocks, scale, 1.0)
        acc = acc * sc
    o_ref[...] = acc.astype(o_ref.dtype)


def matmul(a, w, li, out_dtype, *, n_off=0, n=None, scale=None, scale_cols=None, bm_pref=1024, bn_pref=1024):
    m, k = a.shape
    n = w.shape[2] - n_off if n is None else n
    bm = _blk(m, bm_pref)
    bn = _blk(n, bn_pref)
    while n_off % bn:
        bn //= 2
    assert bn % LANE == 0 or bn == w.shape[2]
    joff = n_off // bn
    scale_blocks = None
    if scale_cols is not None:
        assert scale_cols % bn == 0
        scale_blocks = scale_cols // bn
    return pl.pallas_call(
        functools.partial(_mm_kernel, scale=scale, scale_blocks=scale_blocks),
        grid=(m // bm, n // bn),
        in_specs=[pl.BlockSpec((bm, k), lambda i, j: (i, 0)),
                  pl.BlockSpec((None, k, bn), lambda i, j: (li, 0, j + joff))],
        out_specs=pl.BlockSpec((bm, bn), lambda i, j: (i, j)),
        out_shape=jax.ShapeDtypeStruct((m, n), out_dtype),
        compiler_params=_cparams("parallel", "parallel"),
        name="matmul",
    )(a, w)


def _mm2_kernel(a1_ref, a2_ref, w1_ref, w2_ref, o_ref):
    acc = jnp.dot(a1_ref[...], w1_ref[...], preferred_element_type=F32)
    acc = acc + jnp.dot(a2_ref[...], w2_ref[...], preferred_element_type=F32)
    o_ref[...] = acc.astype(o_ref.dtype)


def matmul_cat2(a1, a2, w, li, out_dtype):
    m, k1 = a1.shape
    assert a2.shape == (m, k1) and w.shape[1] == 2 * k1
    n = w.shape[2]
    bm = _blk(m, 1024)
    bn = _blk(n, 1024)
    return pl.pallas_call(
        _mm2_kernel,
        grid=(m // bm, n // bn),
        in_specs=[pl.BlockSpec((bm, k1), lambda i, j: (i, 0)),
                  pl.BlockSpec((bm, k1), lambda i, j: (i, 0)),
                  pl.BlockSpec((None, k1, bn), lambda i, j: (li, 0, j)),
                  pl.BlockSpec((None, k1, bn), lambda i, j: (li, 1, j))],
        out_specs=pl.BlockSpec((bm, bn), lambda i, j: (i, j)),
        out_shape=jax.ShapeDtypeStruct((m, n), out_dtype),
        compiler_params=_cparams("parallel", "parallel"),
        name="matmul_cat2",
    )(a1, a2, w, w)


def _mm_acc_kernel(a_ref, w_ref, o_ref, acc_ref, *, nk, bk, k_valid):
    kk = pl.program_id(2)
    w = w_ref[...]
    if k_valid < nk * bk:
        row = lax.broadcasted_iota(jnp.int32, w.shape, 0)
        w = jnp.where(row < k_valid - kk * bk, w, jnp.zeros_like(w))
    part = jnp.dot(a_ref[...], w, preferred_element_type=F32)

    @pl.when(kk == 0)
    def _():
        acc_ref[...] = part

    @pl.when(kk > 0)
    def _():
        acc_ref[...] += part

    @pl.when(kk == nk - 1)
    def _():
        o_ref[...] = acc_ref[...].astype(o_ref.dtype)


def matmul_ktiled(a, w, li, out_dtype, *, bk):
    m, k = a.shape
    _, k_valid, n = w.shape
    bm = _blk(m, 1024)
    bn = _blk(n, 1024)
    assert k % bk == 0 and k - bk < k_valid <= k
    nk = k // bk
    return pl.pallas_call(
        functools.partial(_mm_acc_kernel, nk=nk, bk=bk, k_valid=k_valid),
        grid=(m // bm, n // bn, nk),
        in_specs=[pl.BlockSpec((bm, bk), lambda i, j, kk: (i, kk)),
                  pl.BlockSpec((None, bk, bn), lambda i, j, kk: (li, kk, j))],
        out_specs=pl.BlockSpec((bm, bn), lambda i, j, kk: (i, j)),
        out_shape=jax.ShapeDtypeStruct((m, n), out_dtype),
        scratch_shapes=[pltpu.VMEM((bm, bn), F32)],
        compiler_params=_cparams("parallel", "parallel", "arbitrary"),
        name="matmul_ktiled",
    )(a, w)


def _swiglu_kernel(a_ref, wg_ref, wu_ref, o_ref, *, bn, ff):
    a = a_ref[...]
    g = jnp.dot(a, wg_ref[...], preferred_element_type=F32)
    u = jnp.dot(a, wu_ref[...], preferred_element_type=F32)
    act = g * jax.nn.sigmoid(g) * u
    col = pl.program_id(1) * bn + lax.broadcasted_iota(jnp.int32, (1, bn), 1)
    o_ref[...] = jnp.where(col < ff, act, 0.0).astype(o_ref.dtype)


def swiglu_up(h, w_gate, w_up, li, *, ffp):
    m, k = h.shape
    ff = w_gate.shape[2]
    bm = _blk(m, 1024)
    bn = FF_BLOCK
    assert ffp % bn == 0 and ffp - bn < ff <= ffp
    return pl.pallas_call(
        functools.partial(_swiglu_kernel, bn=bn, ff=ff),
        grid=(m // bm, ffp // bn),
        in_specs=[pl.BlockSpec((bm, k), lambda i, j: (i, 0)),
                  pl.BlockSpec((None, k, bn), lambda i, j: (li, 0, j)),
                  pl.BlockSpec((None, k, bn), lambda i, j: (li, 0, j))],
        out_specs=pl.BlockSpec((bm, bn), lambda i, j: (i, j)),
        out_shape=jax.ShapeDtypeStruct((m, ffp), BF16),
        compiler_params=_cparams("parallel", "parallel"),
        name="swiglu_up",
    )(h, w_gate, w_up)


def _conv_kernel(a_ref, b_ref, ah_ref, bh_ref, hist_ref, cw_ref, cb_ref, lg_ref, lb_ref, o_ref,
                 buf_ref, y_ref, *, tb, width, halo, tt):
    t_blk = pl.program_id(1)
    a = a_ref[...].astype(F32)
    u_cur = a * jax.nn.sigmoid(b_ref[...].astype(F32))
    @pl.when(t_blk == 0)
    def _():
        buf_ref[0:halo, :] = hist_ref[...]

    @pl.when(t_blk > 0)
    def _():
        ah = ah_ref[...].astype(F32)
        buf_ref[0:halo, :] = ah * jax.nn.sigmoid(bh_ref[...].astype(F32))

    buf_ref[halo:halo + tb, :] = u_cur
    ch = a.shape[1]
    base = halo - (width - 1)

    sub = SUBLANES

    def lane_tile(ct, carry):
        c0 = pl.multiple_of(ct * LANE, LANE)
        bias = cb_ref[:, pl.ds(c0, LANE)]
        for t0 in range(0, tb, tt):
            acc = jnp.broadcast_to(bias, (tt, LANE)).reshape(tt // sub, sub, LANE)
            for ph in range(sub):
                taps = [w for w in range(width) if (base + w) % sub == ph]
                if not taps:
                    continue
                span = tt + taps[-1] - taps[0]
                blk = buf_ref[pl.ds(base + t0 + taps[0], span), pl.ds(c0, LANE)].reshape(span // sub, sub, LANE)
                for w in taps:
                    off = (w - taps[0]) // sub
                    acc = acc + cw_ref[pl.ds(w * sub, sub), pl.ds(c0, LANE)][None] * blk[off:off + tt // sub]
            y_ref[pl.ds(t0, tt), pl.ds(c0, LANE)] = acc.reshape(tt, LANE)
        return carry

    lax.fori_loop(0, ch // LANE, lane_tile, 0)
    y = y_ref[...]
    mu = jnp.mean(y, axis=-1, keepdims=True)
    yc = y - mu
    var = jnp.mean(yc * yc, axis=-1, keepdims=True)
    z = yc * lax.rsqrt(var + EPS) * lg_ref[...] + lb_ref[...]
    o_ref[...] = (z * jax.nn.sigmoid(z)).astype(o_ref.dtype)


def conv_module(z_main, hist, conv_w, conv_b, ln_g, ln_b, *, nb, t, ch):
    width = conv_w.shape[0]
    halo = hist.shape[1]
    tb = _blk(t, 256)
    tt = _blk(tb, 64)
    nt = t // tb
    hb = tb // halo
    cw = jnp.repeat(conv_w, SUBLANES, axis=0)
    row = lambda c: pl.BlockSpec((tb, ch), lambda b, i: (b * nt + i, c))
    prev = lambda c: pl.BlockSpec((halo, ch), lambda b, i: (jnp.maximum((b * nt + i) * hb - 1, 0), c))
    vec = pl.BlockSpec((1, ch), lambda b, i: (0, 0))
    return pl.pallas_call(
        functools.partial(_conv_kernel, tb=tb, width=width, halo=halo, tt=tt),
        grid=(nb, nt),
        in_specs=[row(0), row(1), prev(0), prev(1),
                  pl.BlockSpec((None, halo, ch), lambda b, i: (b, 0, 0)),
                  pl.BlockSpec((width * SUBLANES, ch), lambda b, i: (0, 0)), vec, vec, vec],
        out_specs=pl.BlockSpec((tb, ch), lambda b, i: (b * nt + i, 0)),
        out_shape=jax.ShapeDtypeStruct((nb * t, ch), BF16),
        scratch_shapes=[pltpu.VMEM((halo + tb, ch), F32), pltpu.VMEM((tb, ch), F32)],
        compiler_params=_cparams("parallel", "parallel"),
        name="conv_module",
    )(z_main, z_main, z_main, z_main, hist, cw, conv_b.reshape(1, ch), ln_g.reshape(1, ch),
      ln_b.reshape(1, ch))


def _glu_tail_kernel(z_ref, o_ref, *, ch, keep):
    z = z_ref[...]
    u = z[:, :ch] * jax.nn.sigmoid(z[:, ch:])
    o_ref[...] = u[u.shape[0] - keep:, :]


def glu_tail(z_tail, *, nb, rows, ch, keep):
    return pl.pallas_call(
        functools.partial(_glu_tail_kernel, ch=ch, keep=keep),
        grid=(nb,),
        in_specs=[pl.BlockSpec((rows, 2 * ch), lambda b: (b, 0))],
        out_specs=pl.BlockSpec((None, keep, ch), lambda b: (b, 0, 0)),
        out_shape=jax.ShapeDtypeStruct((nb, keep, ch), F32),
        compiler_params=_cparams("parallel"),
        name="glu_tail",
    )(z_tail)


def _rope_pairs(v, cc, ss):
    return v * cc + pltpu.roll(v, LANE // 2, axis=1) * ss


def _mla_kv_prep_kernel(z_ref, g_ref, cc_ref, ss_ref, lat_ref, kpe_ref, a_ref, *, kvl):
    z = z_ref[...]
    lat = _rms_f32(z[:, :kvl], g_ref[...])
    rot = _rope_pairs(z[:, kvl:kvl + LANE], cc_ref[...], ss_ref[...])
    lat_ref[...] = lat
    kpe_ref[...] = rot[:, :LANE // 2]
    a_ref[:, :kvl] = lat.astype(a_ref.dtype)
    a_ref[:, kvl:] = rot.astype(a_ref.dtype)


def mla_kv_prep(z_kv, g, cc, ss, *, t):
    m, wdt = z_kv.shape
    kvl = wdt - LANE
    bm = _blk(t, 256)
    nt = t // bm
    row = lambda wd: pl.BlockSpec((bm, wd), lambda i: (i, 0))
    tab = pl.BlockSpec((bm, LANE), lambda i: (i % nt, 0))
    return pl.pallas_call(
        functools.partial(_mla_kv_prep_kernel, kvl=kvl),
        grid=(m // bm,),
        in_specs=[row(wdt), pl.BlockSpec((1, kvl), lambda i: (0, 0)), tab, tab],
        out_specs=[row(kvl), row(LANE // 2), row(wdt)],
        out_shape=[jax.ShapeDtypeStruct((m, kvl), F32), jax.ShapeDtypeStruct((m, LANE // 2), F32),
                   jax.ShapeDtypeStruct((m, wdt), BF16)],
        compiler_params=_cparams("parallel"),
        name="mla_kv_prep",
    )(z_kv, g.reshape(1, kvl), cc, ss)


def _mla_q_kernel(z_ref, g_ref, w_ref, cc_ref, ss_ref, o_ref, *, heads, scale):
    a = _rms_f32(z_ref[...].astype(F32), g_ref[...]).astype(BF16)
    acc = jnp.dot(a, w_ref[...], preferred_element_type=F32)
    cc = cc_ref[...]
    ss = ss_ref[...]
    for h in range(heads):
        c0 = h * MXU_DIM
        o_ref[:, c0:c0 + LANE] = (acc[:, c0:c0 + LANE] * scale).astype(o_ref.dtype)
        rot = _rope_pairs(acc[:, c0 + LANE:c0 + MXU_DIM], cc, ss)
        o_ref[:, c0 + LANE:c0 + MXU_DIM] = (rot * scale).astype(o_ref.dtype)


def mla_q_proj(z_main, col_blk, g, w_uq_ext, cc, ss, *, t, heads, scale):
    m = z_main.shape[0]
    ql, n = w_uq_ext.shape
    bm = _blk(t, 512)
    nt = t // bm
    tab = pl.BlockSpec((bm, LANE), lambda i: (i % nt, 0))
    return pl.pallas_call(
        functools.partial(_mla_q_kernel, heads=heads, scale=scale),
        grid=(m // bm,),
        in_specs=[pl.BlockSpec((bm, ql), lambda i: (i, col_blk)),
                  pl.BlockSpec((1, ql), lambda i: (0, 0)),
                  pl.BlockSpec((ql, n), lambda i: (0, 0)), tab, tab],
        out_specs=pl.BlockSpec((bm, n), lambda i: (i, 0)),
        out_shape=jax.ShapeDtypeStruct((m, n), BF16),
        compiler_params=_cparams("parallel"),
        name="mla_q_proj",
    )(z_main, g.reshape(1, ql), w_uq_ext, cc, ss)


def _mla_attn_kernel(q_ref, k_ref, v_ref, o_ref, m_ref, l_ref, acc_ref, *, heads, dv, bq, bk, nk,
                     mode, kv_len):
    i = pl.program_id(1)
    j = pl.program_id(2)

    @pl.when(j == 0)
    def _():
        m_ref[...] = jnp.full(m_ref.shape, NEG, F32)
        l_ref[...] = jnp.zeros(l_ref.shape, F32)
        acc_ref[...] = jnp.zeros(acc_ref.shape, F32)

    def step(mask):
        for h in range(heads):
            qh = q_ref[:, h * MXU_DIM:(h + 1) * MXU_DIM]
            kh = k_ref[:, h * MXU_DIM:(h + 1) * MXU_DIM]
            s = lax.dot_general(qh, kh, (((1,), (1,)), ((), ())), preferred_element_type=F32)
            if mask is not None:
                s = jnp.where(mask, s, NEG)
            tiles = [s[:, t * LANE:(t + 1) * LANE] for t in range(bk // LANE)]
            mx = tiles[0]
            for tl in tiles[1:]:
                mx = jnp.maximum(mx, tl)
            m_prev = m_ref[h]
            m_new = jnp.maximum(m_prev, jnp.max(mx, axis=1, keepdims=True))
            alpha = jnp.exp2(m_prev - m_new)
            ps = [jnp.exp2(tl - m_new) for tl in tiles]
            lsum = ps[0]
            for p in ps[1:]:
                lsum = lsum + p
            l_ref[h] = alpha * l_ref[h] + lsum
            pb = jnp.concatenate(ps, axis=1).astype(BF16)
            pv = jnp.dot(pb, v_ref[:, h * dv:(h + 1) * dv], preferred_element_type=F32)
            acc_ref[:, h * dv:(h + 1) * dv] = alpha * acc_ref[:, h * dv:(h + 1) * dv] + pv
            m_ref[h] = m_new

    if mode == "block_causal":
        @pl.when(j < i)
        def _():
            step(None)

        @pl.when(j == i)
        def _():
            qc = lax.broadcasted_iota(jnp.int32, (bq, bk), 0) // CHUNK
            kc = lax.broadcasted_iota(jnp.int32, (bq, bk), 1) // CHUNK
            step(kc <= qc)
    else:
        @pl.when(j < nk - 1)
        def _():
            step(None)

        @pl.when(j == nk - 1)
        def _():
            kpos = (nk - 1) * bk + lax.broadcasted_iota(jnp.int32, (bq, bk), 1)
            step(kpos < kv_len if kv_len < nk * bk else None)

    @pl.when(j == nk - 1)
    def _():
        for h in range(heads):
            l = jnp.sum(l_ref[h], axis=1, keepdims=True)
            o_ref[:, h * dv:(h + 1) * dv] = (acc_ref[:, h * dv:(h + 1) * dv] / l).astype(o_ref.dtype)


def mla_attention(q, kv, *, nb, tq, lk, kv_len, heads, dv, mode):
    kw = heads * MXU_DIM
    vw = heads * dv
    assert kw % vw == 0 and dv == LANE
    bq = _blk(tq, 512)
    if mode == "block_causal":
        assert lk == tq and kv_len == lk and bq % CHUNK == 0
        bk = bq
    else:
        bk = _blk(lk, 768)
    nq, nk = tq // bq, lk // bk
    assert kv_len > (nk - 1) * bk, "key padding must sit in the last key block"
    if mode == "block_causal":
        kidx = lambda b, i, j: (b * nk + jnp.minimum(j, i), 0)
        vidx = lambda b, i, j: (b * nk + jnp.minimum(j, i), kw // vw)
    else:
        kidx = lambda b, i, j: (b * nk + j, 0)
        vidx = lambda b, i, j: (b * nk + j, kw // vw)
    return pl.pallas_call(
        functools.partial(_mla_attn_kernel, heads=heads, dv=dv, bq=bq, bk=bk, nk=nk, mode=mode,
                          kv_len=kv_len),
        grid=(nb, nq, nk),
        in_specs=[pl.BlockSpec((bq, kw), lambda b, i, j: (b * nq + i, 0)),
                  pl.BlockSpec((bk, kw), kidx),
                  pl.BlockSpec((bk, vw), vidx)],
        out_specs=pl.BlockSpec((bq, vw), lambda b, i, j: (b * nq + i, 0)),
        out_shape=jax.ShapeDtypeStruct((nb * tq, vw), BF16),
        scratch_shapes=[pltpu.VMEM((heads, bq, LANE), F32), pltpu.VMEM((heads, bq, LANE), F32),
                        pltpu.VMEM((bq, vw), F32)],
        compiler_params=_cparams("parallel", "parallel", "arbitrary"),
        name="mla_attention",
    )(q, kv, kv)


def _band_bias_kernel(t_ref, o_ref, *, rows, max_rel, kb, past):
    t = t_ref[...]
    hi = t.astype(BF16)
    r1 = t - hi.astype(F32)
    mid = r1.astype(BF16)
    lo = (r1 - mid.astype(F32)).astype(BF16)
    nrel = t.shape[1]
    j = lax.broadcasted_iota(jnp.int32, (nrel, kb), 1)
    r = lax.broadcasted_iota(jnp.int32, (nrel, kb), 0)
    kc = lax.broadcasted_iota(jnp.int32, (t.shape[0], kb), 1) // CHUNK - past // CHUNK
    for rr in range(rows):
        i = pl.program_id(0) * rows + rr
        idx = jnp.clip(i - j + past, -max_rel, max_rel) + max_rel
        onehot = jnp.where(r == idx, 1.0, 0.0).astype(BF16)
        vals = (jnp.dot(hi, onehot, preferred_element_type=F32)
                + jnp.dot(mid, onehot, preferred_element_type=F32)
                + jnp.dot(lo, onehot, preferred_element_type=F32))
        qc = i // CHUNK
        ok = (kc <= qc) & (kc >= qc - BAND_CHUNKS)
        o_ref[rr] = jnp.where(ok, vals * LOG2E, NEG)


def band_bias(table):
    heads, nrel = table.shape
    max_rel = (nrel - 1) // 2
    nrel_p = -(-nrel // LANE) * LANE
    tp = jnp.pad(table, ((0, 0), (0, nrel_p - nrel)))
    rows = 8
    past = BAND_CHUNKS * CHUNK
    return pl.pallas_call(
        functools.partial(_band_bias_kernel, rows=rows, max_rel=max_rel, kb=BAND_KB, past=past),
        grid=(BAND_QB // rows,),
        in_specs=[pl.BlockSpec((heads, nrel_p), lambda i: (0, 0))],
        out_specs=pl.BlockSpec((rows, heads, BAND_KB), lambda i: (i, 0, 0)),
        out_shape=jax.ShapeDtypeStruct((BAND_QB, heads, BAND_KB), F32),
        compiler_params=_cparams("parallel"),
        name="band_bias",
    )(tp)


def _softmax_pv(tiles, v_parts):
    mx = tiles[0]
    for tl in tiles[1:]:
        mx = jnp.maximum(mx, tl)
    m = jnp.max(mx, axis=1, keepdims=True)
    ps = [jnp.exp2(tl - m) for tl in tiles]
    lsum = ps[0]
    for p in ps[1:]:
        lsum = lsum + p
    l = jnp.sum(lsum, axis=1, keepdims=True)
    per = len(tiles) // len(v_parts)
    o = None
    for n, vp in enumerate(v_parts):
        pb = jnp.concatenate(ps[n * per:(n + 1) * per], axis=1).astype(BF16)
        part = jnp.dot(pb, vp, preferred_element_type=F32)
        o = part if o is None else o + part
    return o / l


def _lane_tiles(s):
    return [s[:, c:c + LANE] for c in range(0, s.shape[1], LANE)]


def _band_prompt_kernel(q_ref, k0_ref, k1_ref, k2_ref, v0_ref, v1_ref, v2_ref, bias_ref, o_ref, *,
                        group, dh):
    qb = pl.program_id(2)
    pen = [jnp.where(qb < 2, NEG, 0.0), jnp.where(qb < 1, NEG, 0.0), None]
    for g in range(group):
        sl = slice(g * dh, (g + 1) * dh)
        qh = q_ref[:, sl]
        tiles = []
        for n, kr in enumerate((k0_ref, k1_ref, k2_ref)):
            part = lax.dot_general(qh, kr[:, sl], (((1,), (1,)), ((), ())), preferred_element_type=F32)
            for c in range(0, BAND_QB, LANE):
                tl = part[:, c:c + LANE] + bias_ref[g, :, n * BAND_QB + c:n * BAND_QB + c + LANE]
                tiles.append(tl if pen[n] is None else tl + pen[n])
        o = _softmax_pv(tiles, [v0_ref[:, sl], v1_ref[:, sl], v2_ref[:, sl]])
        o_ref[:, sl] = o.astype(o_ref.dtype)


def band_attention_prompt(qkv, bias, *, nb, t, heads, dh):
    d = heads * dh
    group = min(heads, 8)
    ng = heads // group
    gw = group * dh
    nq = t // BAND_QB
    assert t % BAND_QB == 0 and BAND_KB == 3 * BAND_QB

    def kspec(back, col0):
        return pl.BlockSpec((BAND_QB, gw),
                            lambda hg, b, i: (b * nq + jnp.maximum(i - back, 0), col0 + hg))

    return pl.pallas_call(
        functools.partial(_band_prompt_kernel, group=group, dh=dh),
        grid=(ng, nb, nq),
        in_specs=[pl.BlockSpec((BAND_QB, gw), lambda hg, b, i: (b * nq + i, hg)),
                  kspec(2, ng), kspec(1, ng), kspec(0, ng),
                  kspec(2, 2 * ng), kspec(1, 2 * ng), kspec(0, 2 * ng),
                  pl.BlockSpec((group, BAND_QB, BAND_KB), lambda hg, b, i: (hg, 0, 0))],
        out_specs=pl.BlockSpec((BAND_QB, gw), lambda hg, b, i: (b * nq + i, hg)),
        out_shape=jax.ShapeDtypeStruct((nb * t, d), BF16),
        compiler_params=_cparams("parallel", "parallel", "parallel"),
        name="band_attention_prompt",
    )(qkv, qkv, qkv, qkv, qkv, qkv, qkv, bias)


def _band_step_kernel(q_ref, k_ref, v_ref, bias_ref, o_ref, *, group, dh):
    for g in range(group):
        sl = slice(g * dh, (g + 1) * dh)
        s = lax.dot_general(q_ref[:, sl], k_ref[:, sl], (((1,), (1,)), ((), ())),
                            preferred_element_type=F32) + bias_ref[g]
        o = _softmax_pv(_lane_tiles(s), [v_ref[:, sl]])
        o_ref[:, sl] = o.astype(o_ref.dtype)


def band_attention_step(qkv, k_all, v_all, bias, *, nb, t, heads, dh):
    d = heads * dh
    group = min(heads, 8)
    ng = heads // group
    gw = group * dh
    lk = k_all.shape[0] // nb
    return pl.pallas_call(
        functools.partial(_band_step_kernel, group=group, dh=dh),
        grid=(ng, nb),
        in_specs=[pl.BlockSpec((t, gw), lambda hg, b: (b, hg)),
                  pl.BlockSpec((lk, gw), lambda hg, b: (b, hg)),
                  pl.BlockSpec((lk, gw), lambda hg, b: (b, hg)),
                  pl.BlockSpec((group, t, lk), lambda hg, b: (hg, 0, 0))],
        out_specs=pl.BlockSpec((t, gw), lambda hg, b: (b, hg)),
        out_shape=jax.ShapeDtypeStruct((nb * t, d), BF16),
        compiler_params=_cparams("parallel", "parallel"),
        name="band_attention_step",
    )(qkv, k_all, v_all, bias)


def _mem_attn_kernel(q_ref, k_ref, v_ref, o_ref, *, heads, dh):
    for h in range(heads):
        sl = slice(h * dh, (h + 1) * dh)
        s = lax.dot_general(q_ref[:, sl], k_ref[:, sl], (((1,), (1,)), ((), ())),
                            preferred_element_type=F32)
        o = _softmax_pv(_lane_tiles(s), [v_ref[:, sl]])
        o_ref[:, sl] = o.astype(o_ref.dtype)


def mem_attention(q, mk, mv, *, nb, t, heads, dh):
    d = heads * dh
    ml = mk.shape[0] // nb
    bq = _blk(t, 512)
    nq = t // bq
    return pl.pallas_call(
        functools.partial(_mem_attn_kernel, heads=heads, dh=dh),
        grid=(nb, nq),
        in_specs=[pl.BlockSpec((bq, d), lambda b, i: (b * nq + i, 0)),
                  pl.BlockSpec((ml, d), lambda b, i: (b, 0)),
                  pl.BlockSpec((ml, d), lambda b, i: (b, 0))],
        out_specs=pl.BlockSpec((bq, d), lambda b, i: (b * nq + i, 0)),
        out_shape=jax.ShapeDtypeStruct((nb * t, d), BF16),
        compiler_params=_cparams("parallel", "parallel"),
        name="mem_attention",
    )(q, mk, mv)


def _cast_kernel(x_ref, o_ref):
    o_ref[...] = x_ref[...].astype(o_ref.dtype)


def cast_bf16(w, *, col_blk=0, cols=None):
    nl, r, c = w.shape
    cols = c if cols is None else cols
    assert cols == c or cols % LANE == 0
    br = _blk(r, 1 << max(3, (CAST_BLOCK_BYTES // (4 * cols)).bit_length() - 1))
    return pl.pallas_call(
        _cast_kernel,
        grid=(nl, r // br),
        in_specs=[pl.BlockSpec((None, br, cols), lambda l, i: (l, i, col_blk))],
        out_specs=pl.BlockSpec((None, br, cols), lambda l, i: (l, i, 0)),
        out_shape=jax.ShapeDtypeStruct((nl, r, cols), BF16),
        compiler_params=_cparams("parallel", "parallel"),
        name="cast_bf16",
    )(w)


def _prep_weights(W, dims):
    heads, nope, rope, dv, kvl = dims["mla_heads"], dims["nope"], dims["rope"], dims["dv"], dims["kvl"]
    ch, ql = dims["ch"], dims["ql"]
    half = rope // 2
    P = {}
    w_in = W["w_in_ab"]
    s2 = 2 * ch + ql
    P["w_in_main"] = cast_bf16(w_in, col_blk=0, cols=s2)
    kpe0 = s2 + kvl
    P["w_in_kv"] = jnp.concatenate(
        [w_in[:, :, s2:], w_in[:, :, kpe0 + half:kpe0 + rope], w_in[:, :, kpe0:kpe0 + half]],
        axis=-1).astype(BF16)
    ne = w_in.shape[0]
    wq = W["w_uq"].reshape(ne, ql, heads, nope + rope)
    pe = wq[..., nope:]
    pe_sw = jnp.concatenate([pe[..., half:], pe[..., :half]], axis=-1)
    P["w_uq"] = jnp.concatenate([wq[..., :nope], pe, pe_sw], axis=-1).reshape(
        ne, ql, heads * MXU_DIM).astype(BF16)
    wkv = W["w_ukv"].reshape(ne, kvl, heads, nope + dv)
    kslot = jnp.pad(wkv[..., :nope], ((0, 0), (0, LANE), (0, 0), (0, MXU_DIM - nope)))
    eye = jnp.zeros((LANE, MXU_DIM), F32).at[jnp.arange(rope), nope + jnp.arange(rope)].set(1.0)
    kslot = kslot + jnp.pad(eye, ((kvl, 0), (0, 0)))[None, :, None, :]
    vpart = jnp.pad(wkv[..., nope:], ((0, 0), (0, LANE), (0, 0), (0, 0)))
    P["w_ukv"] = jnp.concatenate(
        [kslot.reshape(ne, kvl + LANE, heads * MXU_DIM), vpart.reshape(ne, kvl + LANE, heads * dv)],
        axis=-1).astype(BF16)
    for name in ("w_out_ab", "w_qkv_c", "w_out_c", "w_q_mem", "w_kv_mem", "w_o_mem", "w_down"):
        P[name] = cast_bf16(W[name])
    ff = W["w_down"].shape[1]
    P["w_gate"] = cast_bf16(W["w_gate_up"], col_blk=0, cols=ff)
    P["w_up"] = cast_bf16(W["w_gate_up"], col_blk=1, cols=ff)
    P["ffp"] = -(-ff // FF_BLOCK) * FF_BLOCK
    return P


def _rope_tabs(pos, rope):
    half = rope // 2
    inv = ROPE_THETA ** (-jnp.arange(0, rope, 2, dtype=F32) / rope)
    ang = pos.astype(F32)[:, None] * inv[None, :]
    cos, sin = jnp.cos(ang), jnp.sin(ang)
    zero = jnp.zeros((pos.shape[0], LANE - rope), F32)
    assert 2 * rope == LANE and half * 2 == rope
    return jnp.concatenate([cos, cos, zero], axis=1), jnp.concatenate([-sin, sin, zero], axis=1)


def _trunk(x3, pos0, mem, mem_k_c, mem_v_c, lat_c, kpe_c, conv_c, bk_c, bv_c, W, P, dims, bias_full):
    prompt = mem is not None
    nb, t, d = x3.shape
    m = nb * t
    depth = W["norm_mix_pre"].shape[0]
    ch, ql, kvl, rope = dims["ch"], dims["ql"], dims["kvl"], dims["rope"]
    mh, nope, dv = dims["mla_heads"], dims["nope"], dims["dv"]
    cheads, cdh = dims["c_heads"], dims["c_dh"]
    memh, memdh = dims["mem_heads"], dims["mem_dh"]
    width = W["conv_w"].shape[1]
    keep = width - 1
    halo = 32
    assert keep <= halo and t % halo == 0 and (2 * ch) % ql == 0
    band_keep = BAND_CHUNKS * CHUNK

    x = x3.reshape(m, d)
    cc, ss = _rope_tabs(pos0 + jnp.arange(t, dtype=jnp.int32), rope)
    q_scale = (nope + rope) ** -0.5 * LOG2E
    lats, kpes, convs, bks, bvs, mks, mvs = [], [], [], [], [], [], []

    def tail_rows(h, rows):
        return h.reshape(nb, t, d)[:, t - rows:].reshape(nb * rows, d)

    h = rmsnorm(x, W["norm_mix_pre"][0])
    for l in range(depth):
        if l % 2 == 0:
            e = l // 2
            z_main = matmul(h, P["w_in_main"], e, BF16)
            z_kv = matmul(h, P["w_in_kv"], e, F32)
            if prompt:
                hist = jnp.zeros((nb, halo, ch), F32)
            else:
                hist = jnp.pad(conv_c[e], ((0, 0), (halo - keep, 0), (0, 0)))
            ya = conv_module(z_main, hist, W["conv_w"][e], W["conv_b"][e], W["conv_ln_g"][e],
                             W["conv_ln_b"][e], nb=nb, t=t, ch=ch)
            z_tail = matmul(tail_rows(h, halo), P["w_in_main"], e, F32, n=2 * ch)
            convs.append(glu_tail(z_tail, nb=nb, rows=halo, ch=ch, keep=keep))
            lat, kpe, a_kv = mla_kv_prep(z_kv, W["kv_norm_g"][e], cc, ss, t=t)
            lats.append(lat.reshape(nb, t, kvl))
            kpes.append(kpe.reshape(nb, t, rope))
            q = mla_q_proj(z_main, (2 * ch) // ql, W["q_norm_g"][e], P["w_uq"][e], cc, ss, t=t,
                           heads=mh, scale=q_scale)
            if prompt:
                lk, kv_len, mode = t, t, "block_causal"
                a_all = a_kv
            else:
                past = lat_c.shape[2]
                kv_len = past + t
                assert (kv_len - 1) // CHUNK <= (past // CHUNK), "decode keys must all be visible"
                lk = -(-kv_len // (3 * LANE)) * (3 * LANE)
                a_hist = jnp.concatenate(
                    [lat_c[e], kpe_c[e], jnp.zeros((nb, past, LANE - rope), F32)], axis=-1).astype(BF16)
                a_all = jnp.concatenate(
                    [a_hist, a_kv.reshape(nb, t, kvl + LANE),
                     jnp.zeros((nb, lk - kv_len, kvl + LANE), BF16)], axis=1).reshape(nb * lk, kvl + LANE)
                mode = "all_visible"
            kv = matmul(a_all, P["w_ukv"], e, BF16)
            yb = mla_attention(q, kv, nb=nb, tq=t, lk=lk, kv_len=kv_len, heads=mh, dv=dv, mode=mode)
            y = matmul_cat2(ya, yb, P["w_out_ab"], e, Y_DTYPE)
        else:
            o = l // 2
            dc = cheads * cdh
            qkv = matmul(h, P["w_qkv_c"], o, BF16, scale=cdh ** -0.5 * LOG2E, scale_cols=dc)
            if prompt:
                yc = band_attention_prompt(qkv, bias_full[o], nb=nb, t=t, heads=cheads, dh=cdh)
                rows = min(band_keep, t)
            else:
                k_new = qkv[:, dc:2 * dc].reshape(nb, t, dc)
                v_new = qkv[:, 2 * dc:].reshape(nb, t, dc)
                lh = bk_c.shape[2]
                assert lh == band_keep and t == CHUNK and pos0 % CHUNK == 0
                lkp = -(-(lh + t) // LANE) * LANE
                zpad = jnp.zeros((nb, lkp - lh - t, dc), BF16)
                k_all = jnp.concatenate([bk_c[o].reshape(nb, lh, dc).astype(BF16), k_new, zpad], axis=1)
                v_all = jnp.concatenate([bv_c[o].reshape(nb, lh, dc).astype(BF16), v_new, zpad], axis=1)
                bias_s = jnp.pad(bias_full[o][:, :t, :lh + t], ((0, 0), (0, 0), (0, lkp - lh - t)),
                                 constant_values=NEG)
                yc = band_attention_step(qkv, k_all.reshape(nb * lkp, dc), v_all.reshape(nb * lkp, dc),
                                         bias_s, nb=nb, t=t, heads=cheads, dh=cdh)
                rows = t
            kv_tail = matmul(tail_rows(h, rows), P["w_qkv_c"], o, F32, n_off=dc, n=2 * dc)
            bks.append(kv_tail[:, :dc].reshape(nb, rows, cheads, cdh))
            bvs.append(kv_tail[:, dc:].reshape(nb, rows, cheads, cdh))
            y = matmul(yc, P["w_out_c"], o, Y_DTYPE)
        x, h = resnorm(x, y, W["norm_mix_post"][l], W["norm_mem_pre"][l])

        md = memh * memdh
        if prompt:
            ml = mem.shape[1]
            hm = rmsnorm(mem.reshape(nb * ml, d), W["norm_memkv"][l])
            mkv = matmul(hm, P["w_kv_mem"], l, F32)
            mk32, mv32 = mkv[:, :md], mkv[:, md:]
            mks.append(mk32.reshape(nb, ml, memh, memdh))
            mvs.append(mv32.reshape(nb, ml, memh, memdh))
        else:
            ml = mem_k_c.shape[2]
            mk32 = mem_k_c[l].reshape(nb * ml, md)
            mv32 = mem_v_c[l].reshape(nb * ml, md)
        qm = matmul(h, P["w_q_mem"], l, BF16, scale=memdh ** -0.5 * LOG2E)
        om = mem_attention(qm, mk32.astype(BF16), mv32.astype(BF16), nb=nb, t=t, heads=memh, dh=memdh)
        y = matmul(om, P["w_o_mem"], l, Y_DTYPE)
        x, h = resnorm(x, y, W["norm_mem_post"][l], W["norm_ffn_pre"][l])

        act = swiglu_up(h, P["w_gate"], P["w_up"], l, ffp=P["ffp"])
        y = matmul_ktiled(act, P["w_down"], l, Y_DTYPE, bk=P["ffp"] // 4)
        g_next = W["norm_mix_pre"][l + 1] if l + 1 < depth else None
        x, h = resnorm(x, y, W["norm_ffn_post"][l], g_next)

    st = lambda xs: jnp.stack(xs, 0)
    return (x.reshape(nb, t, d), st(lats), st(kpes), st(convs), st(bks), st(bvs),
            st(mks) if prompt else None, st(mvs) if prompt else None)


def kernel(x_prompt, x_sample, cache_mla_latent, cache_mla_kpe, state_conv, cache_band_k,
           cache_band_v, cache_mem_k, cache_mem_v, mem_prompt, norm_mix_pre, norm_mix_post,
           norm_mem_pre, norm_mem_post, norm_memkv, norm_ffn_pre, norm_ffn_post, w_in_ab,
           conv_w, conv_b, conv_ln_g, conv_ln_b, q_norm_g, w_uq, kv_norm_g, w_ukv, w_out_ab,
           w_qkv_c, rel_bias_c, w_out_c, w_q_mem, w_kv_mem, w_o_mem, w_gate_up, w_down):
    W = dict(norm_mix_pre=norm_mix_pre, norm_mix_post=norm_mix_post, norm_mem_pre=norm_mem_pre,
             norm_mem_post=norm_mem_post, norm_memkv=norm_memkv, norm_ffn_pre=norm_ffn_pre,
             norm_ffn_post=norm_ffn_post, w_in_ab=w_in_ab, conv_w=conv_w, conv_b=conv_b,
             conv_ln_g=conv_ln_g, conv_ln_b=conv_ln_b, q_norm_g=q_norm_g, w_uq=w_uq,
             kv_norm_g=kv_norm_g, w_ukv=w_ukv, w_out_ab=w_out_ab, w_qkv_c=w_qkv_c,
             rel_bias_c=rel_bias_c, w_out_c=w_out_c, w_q_mem=w_q_mem, w_kv_mem=w_kv_mem,
             w_o_mem=w_o_mem, w_gate_up=w_gate_up, w_down=w_down)
    ch = conv_w.shape[2]
    ql = q_norm_g.shape[1]
    kvl = kv_norm_g.shape[1]
    rope = cache_mla_kpe.shape[3]
    c_heads, c_dh = cache_band_k.shape[3], cache_band_k.shape[4]
    mem_heads, mem_dh = cache_mem_k.shape[3], cache_mem_k.shape[4]
    a_cols, b_cols, v_cols = w_uq.shape[2], w_ukv.shape[2], w_out_ab.shape[1] - ch
    mla_heads = (a_cols - b_cols + v_cols) // rope
    dims = dict(ch=ch, ql=ql, kvl=kvl, rope=rope, mla_heads=mla_heads,
                nope=(b_cols - v_cols) // mla_heads, dv=v_cols // mla_heads,
                c_heads=c_heads, c_dh=c_dh, mem_heads=mem_heads, mem_dh=mem_dh)
    assert dims["nope"] == LANE and dims["dv"] == LANE and rope == LANE // 2

    P = _prep_weights(W, dims)
    bias_full = jnp.stack([jnp.transpose(band_bias(rel_bias_c[o]), (1, 0, 2))
                           for o in range(rel_bias_c.shape[0])], 0)

    past = cache_mla_latent.shape[2]
    y_p, lat_p, kpe_p, conv_p, bk_p, bv_p, mk_p, mv_p = _trunk(
        x_prompt, 0, mem_prompt, None, None, None, None, None, None, None, W, P, dims, bias_full)
    y_s, lat_s, kpe_s, conv_s, bk_s, bv_s, _, _ = _trunk(
        x_sample, past, None, cache_mem_k, cache_mem_v, cache_mla_latent, cache_mla_kpe,
        state_conv, cache_band_k, cache_band_v, W, P, dims, bias_full)
    return (y_p, y_s, lat_p, kpe_p, conv_p, bk_p, bv_p, mk_p, mv_p, lat_s, kpe_s, conv_s, bk_s, bv_s)
```

```python
import functools

import jax
import jax.numpy as jnp
from jax import lax
from jax.experimental import pallas as pl
from jax.experimental.pallas import tpu as pltpu

F32 = jnp.float32
BF16 = jnp.bfloat16
Y_DTYPE = BF16

EPS = 1e-6
NEG = -1e30
CHUNK = 64
BAND_CHUNKS = 8
ROPE_THETA = 10000.0
LOG2E = 1.4426950408889634

LANE = 128
SUBLANES = 8
MXU_DIM = 256
VMEM_LIMIT_BYTES = 56 * 1024 * 1024
FF_BLOCK = 512
CAST_BLOCK_BYTES = 8 * 1024 * 1024
MEM_ROWS = 256

BAND_QB = 4 * CHUNK
BAND_KB = BAND_CHUNKS * CHUNK + BAND_QB


def _cparams(*sem):
    return pltpu.CompilerParams(dimension_semantics=sem, vmem_limit_bytes=VMEM_LIMIT_BYTES)


def _blk(n, pref):
    if n <= pref:
        return n
    b = pref
    while n % b:
        b //= 2
    assert b >= 8, (n, pref)
    return b


def _rms_f32(x, g):
    return x * lax.rsqrt(jnp.mean(x * x, axis=-1, keepdims=True) + EPS) * g


def _rmsnorm_kernel(x_ref, g_ref, o_ref):
    o_ref[...] = _rms_f32(x_ref[...].astype(F32), g_ref[...]).astype(o_ref.dtype)


def rmsnorm(x, g, out_dtype=BF16):
    m, d = x.shape
    bm = _blk(m, 256)
    return pl.pallas_call(
        _rmsnorm_kernel,
        grid=(m // bm,),
        in_specs=[pl.BlockSpec((bm, d), lambda i: (i, 0)), pl.BlockSpec((1, d), lambda i: (0, 0))],
        out_specs=pl.BlockSpec((bm, d), lambda i: (i, 0)),
        out_shape=jax.ShapeDtypeStruct((m, d), out_dtype),
        compiler_params=_cparams("parallel"),
        name="rmsnorm",
    )(x, g.reshape(1, d))


def _resnorm_kernel(x_ref, y_ref, gp_ref, gn_ref, xo_ref, h_ref):
    xn = x_ref[...] + _rms_f32(y_ref[...].astype(F32), gp_ref[...])
    xo_ref[...] = xn
    h_ref[...] = _rms_f32(xn, gn_ref[...]).astype(h_ref.dtype)


def _resnorm_last_kernel(x_ref, y_ref, gp_ref, xo_ref):
    xo_ref[...] = x_ref[...] + _rms_f32(y_ref[...].astype(F32), gp_ref[...])


def resnorm(x, y, g_post, g_next):
    m, d = x.shape
    bm = _blk(m, 256)
    row = pl.BlockSpec((bm, d), lambda i: (i, 0))
    vec = pl.BlockSpec((1, d), lambda i: (0, 0))
    if g_next is None:
        return pl.pallas_call(
            _resnorm_last_kernel, grid=(m // bm,), in_specs=[row, row, vec], out_specs=row,
            out_shape=jax.ShapeDtypeStruct((m, d), F32), compiler_params=_cparams("parallel"),
            name="resnorm_last",
        )(x, y, g_post.reshape(1, d)), None
    return pl.pallas_call(
        _resnorm_kernel, grid=(m // bm,), in_specs=[row, row, vec, vec], out_specs=[row, row],
        out_shape=[jax.ShapeDtypeStruct((m, d), F32), jax.ShapeDtypeStruct((m, d), BF16)],
        compiler_params=_cparams("parallel"), name="resnorm",
    )(x, y, g_post.reshape(1, d), g_next.reshape(1, d))


def _mm_kernel(a_ref, w_ref, o_ref, *, scale, scale_blocks):
    acc = jnp.dot(a_ref[...], w_ref[...], preferred_element_type=F32)
    if scale is not None:
        sc = scale if scale_blocks is None else jnp.where(pl.program_id(1) < scale_blocks, scale, 1.0)
        acc = acc * sc
    o_ref[...] = acc.astype(o_ref.dtype)


def matmul(a, w, li, out_dtype, *, n_off=0, n=None, scale=None, scale_cols=None, bm_pref=1024, bn_pref=1024):
    m, k = a.shape
    n = w.shape[2] - n_off if n is None else n
    bm = _blk(m, bm_pref)
    bn = _blk(n, bn_pref)
    while n_off % bn:
        bn //= 2
    assert bn % LANE == 0 or bn == w.shape[2]
    joff = n_off // bn
    scale_blocks = None
    if scale_cols is not None:
        assert scale_cols % bn == 0
        scale_blocks = scale_cols // bn
    return pl.pallas_call(
        functools.partial(_mm_kernel, scale=scale, scale_blocks=scale_blocks),
        grid=(m // bm, n // bn),
        in_specs=[pl.BlockSpec((bm, k), lambda i, j: (i, 0)),
                  pl.BlockSpec((None, k, bn), lambda i, j: (li, 0, j + joff))],
        out_specs=pl.BlockSpec((bm, bn), lambda i, j: (i, j)),
        out_shape=jax.ShapeDtypeStruct((m, n), out_dtype),
        compiler_params=_cparams("parallel", "parallel"),
        name="matmul",
    )(a, w)


def _mm2_kernel(a1_ref, a2_ref, w1_ref, w2_ref, o_ref):
    acc = jnp.dot(a1_ref[...], w1_ref[...], preferred_element_type=F32)
    acc = acc + jnp.dot(a2_ref[...], w2_ref[...], preferred_element_type=F32)
    o_ref[...] = acc.astype(o_ref.dtype)


def matmul_cat2(a1, a2, w, li, out_dtype):
    m, k1 = a1.shape
    assert a2.shape == (m, k1) and w.shape[1] == 2 * k1
    n = w.shape[2]
    bm = _blk(m, 1024)
    bn = _blk(n, 1024)
    return pl.pallas_call(
        _mm2_kernel,
        grid=(m // bm, n // bn),
        in_specs=[pl.BlockSpec((bm, k1), lambda i, j: (i, 0)),
                  pl.BlockSpec((bm, k1), lambda i, j: (i, 0)),
                  pl.BlockSpec((None, k1, bn), lambda i, j: (li, 0, j)),
                  pl.BlockSpec((None, k1, bn), lambda i, j: (li, 1, j))],
        out_specs=pl.BlockSpec((bm, bn), lambda i, j: (i, j)),
        out_shape=jax.ShapeDtypeStruct((m, n), out_dtype),
        compiler_params=_cparams("parallel", "parallel"),
        name="matmul_cat2",
    )(a1, a2, w, w)


def _mm_acc_kernel(a_ref, w_ref, o_ref, acc_ref, *, nk, bk, k_valid):
    kk = pl.program_id(2)
    w = w_ref[...]
    if k_valid < nk * bk:
        row = lax.broadcasted_iota(jnp.int32, w.shape, 0)
        w = jnp.where(row < k_valid - kk * bk, w, jnp.zeros_like(w))
    part = jnp.dot(a_ref[...], w, preferred_element_type=F32)

    @pl.when(kk == 0)
    def _():
        acc_ref[...] = part

    @pl.when(kk > 0)
    def _():
        acc_ref[...] += part

    @pl.when(kk == nk - 1)
    def _():
        o_ref[...] = acc_ref[...].astype(o_ref.dtype)


def matmul_ktiled(a, w, li, out_dtype, *, bk):
    m, k = a.shape
    _, k_valid, n = w.shape
    bm = _blk(m, 1024)
    bn = _blk(n, 1024)
    assert k % bk == 0 and k - bk < k_valid <= k
    nk = k // bk
    return pl.pallas_call(
        functools.partial(_mm_acc_kernel, nk=nk, bk=bk, k_valid=k_valid),
        grid=(m // bm, n // bn, nk),
        in_specs=[pl.BlockSpec((bm, bk), lambda i, j, kk: (i, kk)),
                  pl.BlockSpec((None, bk, bn), lambda i, j, kk: (li, kk, j))],
        out_specs=pl.BlockSpec((bm, bn), lambda i, j, kk: (i, j)),
        out_shape=jax.ShapeDtypeStruct((m, n), out_dtype),
        scratch_shapes=[pltpu.VMEM((bm, bn), F32)],
        compiler_params=_cparams("parallel", "parallel", "arbitrary"),
        name="matmul_ktiled",
    )(a, w)


def _swiglu_kernel(a_ref, wg_ref, wu_ref, o_ref, *, bn, ff):
    a = a_ref[...]
    g = jnp.dot(a, wg_ref[...], preferred_element_type=F32)
    u = jnp.dot(a, wu_ref[...], preferred_element_type=F32)
    act = g * jax.nn.sigmoid(g) * u
    col = pl.program_id(1) * bn + lax.broadcasted_iota(jnp.int32, (1, bn), 1)
    o_ref[...] = jnp.where(col < ff, act, 0.0).astype(o_ref.dtype)


def swiglu_up(h, w_gate, w_up, li, *, ffp):
    m, k = h.shape
    ff = w_gate.shape[2]
    bm = _blk(m, 1024)
    bn = FF_BLOCK
    assert ffp % bn == 0 and ffp - bn < ff <= ffp
    return pl.pallas_call(
        functools.partial(_swiglu_kernel, bn=bn, ff=ff),
        grid=(m // bm, ffp // bn),
        in_specs=[pl.BlockSpec((bm, k), lambda i, j: (i, 0)),
                  pl.BlockSpec((None, k, bn), lambda i, j: (li, 0, j)),
                  pl.BlockSpec((None, k, bn), lambda i, j: (li, 0, j))],
        out_specs=pl.BlockSpec((bm, bn), lambda i, j: (i, j)),
        out_shape=jax.ShapeDtypeStruct((m, ffp), BF16),
        compiler_params=_cparams("parallel", "parallel"),
        name="swiglu_up",
    )(h, w_gate, w_up)


def _conv_kernel(a_ref, b_ref, ah_ref, bh_ref, hist_ref, cw_ref, cb_ref, lg_ref, lb_ref, o_ref,
                 buf_ref, y_ref, *, tb, width, halo, tt):
    t_blk = pl.program_id(1)
    a = a_ref[...].astype(F32)
    u_cur = a * jax.nn.sigmoid(b_ref[...].astype(F32))
    @pl.when(t_blk == 0)
    def _():
        buf_ref[0:halo, :] = hist_ref[...]

    @pl.when(t_blk > 0)
    def _():
        ah = ah_ref[...].astype(F32)
        buf_ref[0:halo, :] = ah * jax.nn.sigmoid(bh_ref[...].astype(F32))

    buf_ref[halo:halo + tb, :] = u_cur
    ch = a.shape[1]
    base = halo - (width - 1)

    sub = SUBLANES

    def lane_tile(ct, carry):
        c0 = pl.multiple_of(ct * LANE, LANE)
        bias = cb_ref[:, pl.ds(c0, LANE)]
        for t0 in range(0, tb, tt):
            acc = jnp.broadcast_to(bias, (tt, LANE)).reshape(tt // sub, sub, LANE)
            for ph in range(sub):
                taps = [w for w in range(width) if (base + w) % sub == ph]
                if not taps:
                    continue
                span = tt + taps[-1] - taps[0]
                blk = buf_ref[pl.ds(base + t0 + taps[0], span), pl.ds(c0, LANE)].reshape(span // sub, sub, LANE)
                for w in taps:
                    off = (w - taps[0]) // sub
                    acc = acc + cw_ref[pl.ds(w * sub, sub), pl.ds(c0, LANE)][None] * blk[off:off + tt // sub]
            y_ref[pl.ds(t0, tt), pl.ds(c0, LANE)] = acc.reshape(tt, LANE)
        return carry

    lax.fori_loop(0, ch // LANE, lane_tile, 0)
    y = y_ref[...]
    mu = jnp.mean(y, axis=-1, keepdims=True)
    yc = y - mu
    var = jnp.mean(yc * yc, axis=-1, keepdims=True)
    z = yc * lax.rsqrt(var + EPS) * lg_ref[...] + lb_ref[...]
    o_ref[...] = (z * jax.nn.sigmoid(z)).astype(o_ref.dtype)


def conv_module(z_main, hist, conv_w, conv_b, ln_g, ln_b, *, nb, t, ch):
    width = conv_w.shape[0]
    halo = hist.shape[1]
    tb = _blk(t, 256)
    tt = _blk(tb, 64)
    nt = t // tb
    hb = tb // halo
    cw = jnp.repeat(conv_w, SUBLANES, axis=0)
    row = lambda c: pl.BlockSpec((tb, ch), lambda b, i: (b * nt + i, c))
    prev = lambda c: pl.BlockSpec((halo, ch), lambda b, i: (jnp.maximum((b * nt + i) * hb - 1, 0), c))
    vec = pl.BlockSpec((1, ch), lambda b, i: (0, 0))
    return pl.pallas_call(
        functools.partial(_conv_kernel, tb=tb, width=width, halo=halo, tt=tt),
        grid=(nb, nt),
        in_specs=[row(0), row(1), prev(0), prev(1),
                  pl.BlockSpec((None, halo, ch), lambda b, i: (b, 0, 0)),
                  pl.BlockSpec((width * SUBLANES, ch), lambda b, i: (0, 0)), vec, vec, vec],
        out_specs=pl.BlockSpec((tb, ch), lambda b, i: (b * nt + i, 0)),
        out_shape=jax.ShapeDtypeStruct((nb * t, ch), BF16),
        scratch_shapes=[pltpu.VMEM((halo + tb, ch), F32), pltpu.VMEM((tb, ch), F32)],
        compiler_params=_cparams("parallel", "parallel"),
        name="conv_module",
    )(z_main, z_main, z_main, z_main, hist, cw, conv_b.reshape(1, ch), ln_g.reshape(1, ch),
      ln_b.reshape(1, ch))


def _glu_tail_kernel(z_ref, o_ref, *, ch, keep):
    z = z_ref[...]
    u = z[:, :ch] * jax.nn.sigmoid(z[:, ch:])
    o_ref[...] = u[u.shape[0] - keep:, :]


def glu_tail(z_tail, *, nb, rows, ch, keep):
    return pl.pallas_call(
        functools.partial(_glu_tail_kernel, ch=ch, keep=keep),
        grid=(nb,),
        in_specs=[pl.BlockSpec((rows, 2 * ch), lambda b: (b, 0))],
        out_specs=pl.BlockSpec((None, keep, ch), lambda b: (b, 0, 0)),
        out_shape=jax.ShapeDtypeStruct((nb, keep, ch), F32),
        compiler_params=_cparams("parallel"),
        name="glu_tail",
    )(z_tail)


def _rope_pairs(v, cc, ss):
    return v * cc + pltpu.roll(v, LANE // 2, axis=1) * ss


def _mla_kv_prep_kernel(z_ref, g_ref, cc_ref, ss_ref, lat_ref, kpe_ref, a_ref, *, kvl):
    z = z_ref[...]
    lat = _rms_f32(z[:, :kvl], g_ref[...])
    rot = _rope_pairs(z[:, kvl:kvl + LANE], cc_ref[...], ss_ref[...])
    lat_ref[...] = lat
    kpe_ref[...] = rot[:, :LANE // 2]
    a_ref[:, :kvl] = lat.astype(a_ref.dtype)
    a_ref[:, kvl:] = rot.astype(a_ref.dtype)


def mla_kv_prep(z_kv, g, cc, ss, *, t):
    m, wdt = z_kv.shape
    kvl = wdt - LANE
    bm = _blk(t, 256)
    nt = t // bm
    row = lambda wd: pl.BlockSpec((bm, wd), lambda i: (i, 0))
    tab = pl.BlockSpec((bm, LANE), lambda i: (i % nt, 0))
    return pl.pallas_call(
        functools.partial(_mla_kv_prep_kernel, kvl=kvl),
        grid=(m // bm,),
        in_specs=[row(wdt), pl.BlockSpec((1, kvl), lambda i: (0, 0)), tab, tab],
        out_specs=[row(kvl), row(LANE // 2), row(wdt)],
        out_shape=[jax.ShapeDtypeStruct((m, kvl), F32), jax.ShapeDtypeStruct((m, LANE // 2), F32),
                   jax.ShapeDtypeStruct((m, wdt), BF16)],
        compiler_params=_cparams("parallel"),
        name="mla_kv_prep",
    )(z_kv, g.reshape(1, kvl), cc, ss)


def _mla_q_kernel(z_ref, g_ref, w_ref, cc_ref, ss_ref, o_ref, *, heads, scale):
    a = _rms_f32(z_ref[...].astype(F32), g_ref[...]).astype(BF16)
    acc = jnp.dot(a, w_ref[...], preferred_element_type=F32)
    cc = cc_ref[...]
    ss = ss_ref[...]
    for h in range(heads):
        c0 = h * MXU_DIM
        o_ref[:, c0:c0 + LANE] = (acc[:, c0:c0 + LANE] * scale).astype(o_ref.dtype)
        rot = _rope_pairs(acc[:, c0 + LANE:c0 + MXU_DIM], cc, ss)
        o_ref[:, c0 + LANE:c0 + MXU_DIM] = (rot * scale).astype(o_ref.dtype)


def mla_q_proj(z_main, col_blk, g, w_uq_ext, cc, ss, *, t, heads, scale):
    m = z_main.shape[0]
    ql, n = w_uq_ext.shape
    bm = _blk(t, 512)
    nt = t // bm
    tab = pl.BlockSpec((bm, LANE), lambda i: (i % nt, 0))
    return pl.pallas_call(
        functools.partial(_mla_q_kernel, heads=heads, scale=scale),
        grid=(m // bm,),
        in_specs=[pl.BlockSpec((bm, ql), lambda i: (i, col_blk)),
                  pl.BlockSpec((1, ql), lambda i: (0, 0)),
                  pl.BlockSpec((ql, n), lambda i: (0, 0)), tab, tab],
        out_specs=pl.BlockSpec((bm, n), lambda i: (i, 0)),
        out_shape=jax.ShapeDtypeStruct((m, n), BF16),
        compiler_params=_cparams("parallel"),
        name="mla_q_proj",
    )(z_main, g.reshape(1, ql), w_uq_ext, cc, ss)


def _mla_attn_kernel(q_ref, k_ref, v_ref, o_ref, m_ref, l_ref, acc_ref, *, heads, dv, bq, bk, nk,
                     mode, kv_len):
    i = pl.program_id(1)
    j = pl.program_id(2)

    @pl.when(j == 0)
    def _():
        m_ref[...] = jnp.full(m_ref.shape, NEG, F32)
        l_ref[...] = jnp.zeros(l_ref.shape, F32)
        acc_ref[...] = jnp.zeros(acc_ref.shape, F32)

    def step(mask):
        for h in range(heads):
            qh = q_ref[:, h * MXU_DIM:(h + 1) * MXU_DIM]
            kh = k_ref[:, h * MXU_DIM:(h + 1) * MXU_DIM]
            s = lax.dot_general(qh, kh, (((1,), (1,)), ((), ())), preferred_element_type=F32)
            if mask is not None:
                s = jnp.where(mask, s, NEG)
            tiles = [s[:, t * LANE:(t + 1) * LANE] for t in range(bk // LANE)]
            mx = tiles[0]
            for tl in tiles[1:]:
                mx = jnp.maximum(mx, tl)
            m_prev = m_ref[h]
            m_new = jnp.maximum(m_prev, jnp.max(mx, axis=1, keepdims=True))
            alpha = jnp.exp2(m_prev - m_new)
            ps = [jnp.exp2(tl - m_new) for tl in tiles]
            lsum = ps[0]
            for p in ps[1:]:
                lsum = lsum + p
            l_ref[h] = alpha * l_ref[h] + lsum
            pb = jnp.concatenate(ps, axis=1).astype(BF16)
            pv = jnp.dot(pb, v_ref[:, h * dv:(h + 1) * dv], preferred_element_type=F32)
            acc_ref[:, h * dv:(h + 1) * dv] = alpha * acc_ref[:, h * dv:(h + 1) * dv] + pv
            m_ref[h] = m_new

    if mode == "block_causal":
        @pl.when(j < i)
        def _():
            step(None)

        @pl.when(j == i)
        def _():
            qc = lax.broadcasted_iota(jnp.int32, (bq, bk), 0) // CHUNK
            kc = lax.broadcasted_iota(jnp.int32, (bq, bk), 1) // CHUNK
            step(kc <= qc)
    else:
        @pl.when(j < nk - 1)
        def _():
            step(None)

        @pl.when(j == nk - 1)
        def _():
            kpos = (nk - 1) * bk + lax.broadcasted_iota(jnp.int32, (bq, bk), 1)
            step(kpos < kv_len if kv_len < nk * bk else None)

    @pl.when(j == nk - 1)
    def _():
        for h in range(heads):
            l = jnp.sum(l_ref[h], axis=1, keepdims=True)
            o_ref[:, h * dv:(h + 1) * dv] = (acc_ref[:, h * dv:(h + 1) * dv] / l).astype(o_ref.dtype)


def mla_attention(q, kv, *, nb, tq, lk, kv_len, heads, dv, mode):
    kw = heads * MXU_DIM
    vw = heads * dv
    assert kw % vw == 0 and dv == LANE
    bq = _blk(tq, 512)
    if mode == "block_causal":
        assert lk == tq and kv_len == lk and bq % CHUNK == 0
        bk = bq
    else:
        bk = _blk(lk, 768)
    nq, nk = tq // bq, lk // bk
    assert kv_len > (nk - 1) * bk, "key padding must sit in the last key block"
    if mode == "block_causal":
        kidx = lambda b, i, j: (b * nk + jnp.minimum(j, i), 0)
        vidx = lambda b, i, j: (b * nk + jnp.minimum(j, i), kw // vw)
    else:
        kidx = lambda b, i, j: (b * nk + j, 0)
        vidx = lambda b, i, j: (b * nk + j, kw // vw)
    return pl.pallas_call(
        functools.partial(_mla_attn_kernel, heads=heads, dv=dv, bq=bq, bk=bk, nk=nk, mode=mode,
                          kv_len=kv_len),
        grid=(nb, nq, nk),
        in_specs=[pl.BlockSpec((bq, kw), lambda b, i, j: (b * nq + i, 0)),
                  pl.BlockSpec((bk, kw), kidx),
                  pl.BlockSpec((bk, vw), vidx)],
        out_specs=pl.BlockSpec((bq, vw), lambda b, i, j: (b * nq + i, 0)),
        out_shape=jax.ShapeDtypeStruct((nb * tq, vw), BF16),
        scratch_shapes=[pltpu.VMEM((heads, bq, LANE), F32), pltpu.VMEM((heads, bq, LANE), F32),
                        pltpu.VMEM((bq, vw), F32)],
        compiler_params=_cparams("parallel", "parallel", "arbitrary"),
        name="mla_attention",
    )(q, kv, kv)


def _band_bias_kernel(t_ref, o_ref, *, rows, max_rel, kb, past):
    t = t_ref[...]
    hi = t.astype(BF16)
    r1 = t - hi.astype(F32)
    mid = r1.astype(BF16)
    lo = (r1 - mid.astype(F32)).astype(BF16)
    nrel = t.shape[1]
    j = lax.broadcasted_iota(jnp.int32, (nrel, kb), 1)
    r = lax.broadcasted_iota(jnp.int32, (nrel, kb), 0)
    kc = lax.broadcasted_iota(jnp.int32, (t.shape[0], kb), 1) // CHUNK - past // CHUNK
    for rr in range(rows):
        i = pl.program_id(0) * rows + rr
        idx = jnp.clip(i - j + past, -max_rel, max_rel) + max_rel
        onehot = jnp.where(r == idx, 1.0, 0.0).astype(BF16)
        vals = (jnp.dot(hi, onehot, preferred_element_type=F32)
                + jnp.dot(mid, onehot, preferred_element_type=F32)
                + jnp.dot(lo, onehot, preferred_element_type=F32))
        qc = i // CHUNK
        ok = (kc <= qc) & (kc >= qc - BAND_CHUNKS)
        o_ref[rr] = jnp.where(ok, vals * LOG2E, NEG)


def band_bias(table):
    heads, nrel = table.shape
    max_rel = (nrel - 1) // 2
    nrel_p = -(-nrel // LANE) * LANE
    tp = jnp.pad(table, ((0, 0), (0, nrel_p - nrel)))
    rows = 8
    past = BAND_CHUNKS * CHUNK
    return pl.pallas_call(
        functools.partial(_band_bias_kernel, rows=rows, max_rel=max_rel, kb=BAND_KB, past=past),
        grid=(BAND_QB // rows,),
        in_specs=[pl.BlockSpec((heads, nrel_p), lambda i: (0, 0))],
        out_specs=pl.BlockSpec((rows, heads, BAND_KB), lambda i: (i, 0, 0)),
        out_shape=jax.ShapeDtypeStruct((BAND_QB, heads, BAND_KB), F32),
        compiler_params=_cparams("parallel"),
        name="band_bias",
    )(tp)


def _softmax_pv(tiles, v_parts):
    mx = tiles[0]
    for tl in tiles[1:]:
        mx = jnp.maximum(mx, tl)
    m = jnp.max(mx, axis=1, keepdims=True)
    ps = [jnp.exp2(tl - m) for tl in tiles]
    lsum = ps[0]
    for p in ps[1:]:
        lsum = lsum + p
    l = jnp.sum(lsum, axis=1, keepdims=True)
    per = len(tiles) // len(v_parts)
    o = None
    for n, vp in enumerate(v_parts):
        pb = jnp.concatenate(ps[n * per:(n + 1) * per], axis=1).astype(BF16)
        part = jnp.dot(pb, vp, preferred_element_type=F32)
        o = part if o is None else o + part
    return o / l


def _lane_tiles(s):
    return [s[:, c:c + LANE] for c in range(0, s.shape[1], LANE)]


def _band_prompt_kernel(q_ref, k0_ref, k1_ref, k2_ref, v0_ref, v1_ref, v2_ref, bias_ref, o_ref, *,
                        group, dh):
    qb = pl.program_id(2)
    pen = [jnp.where(qb < 2, NEG, 0.0), jnp.where(qb < 1, NEG, 0.0), None]
    for g in range(group):
        sl = slice(g * dh, (g + 1) * dh)
        qh = q_ref[:, sl]
        tiles = []
        for n, kr in enumerate((k0_ref, k1_ref, k2_ref)):
            part = lax.dot_general(qh, kr[:, sl], (((1,), (1,)), ((), ())), preferred_element_type=F32)
            for c in range(0, BAND_QB, LANE):
                tl = part[:, c:c + LANE] + bias_ref[g, :, n * BAND_QB + c:n * BAND_QB + c + LANE]
                tiles.append(tl if pen[n] is None else tl + pen[n])
        o = _softmax_pv(tiles, [v0_ref[:, sl], v1_ref[:, sl], v2_ref[:, sl]])
        o_ref[:, sl] = o.astype(o_ref.dtype)


def band_attention_prompt(qkv, bias, *, nb, t, heads, dh):
    d = heads * dh
    group = min(heads, 8)
    ng = heads // group
    gw = group * dh
    nq = t // BAND_QB
    assert t % BAND_QB == 0 and BAND_KB == 3 * BAND_QB

    def kspec(back, col0):
        return pl.BlockSpec((BAND_QB, gw),
                            lambda hg, b, i: (b * nq + jnp.maximum(i - back, 0), col0 + hg))

    return pl.pallas_call(
        functools.partial(_band_prompt_kernel, group=group, dh=dh),
        grid=(ng, nb, nq),
        in_specs=[pl.BlockSpec((BAND_QB, gw), lambda hg, b, i: (b * nq + i, hg)),
                  kspec(2, ng), kspec(1, ng), kspec(0, ng),
                  kspec(2, 2 * ng), kspec(1, 2 * ng), kspec(0, 2 * ng),
                  pl.BlockSpec((group, BAND_QB, BAND_KB), lambda hg, b, i: (hg, 0, 0))],
        out_specs=pl.BlockSpec((BAND_QB, gw), lambda hg, b, i: (b * nq + i, hg)),
        out_shape=jax.ShapeDtypeStruct((nb * t, d), BF16),
        compiler_params=_cparams("parallel", "parallel", "parallel"),
        name="band_attention_prompt",
    )(qkv, qkv, qkv, qkv, qkv, qkv, qkv, bias)


def _band_step_kernel(q_ref, k_ref, v_ref, bias_ref, o_ref, *, group, dh):
    for g in range(group):
        sl = slice(g * dh, (g + 1) * dh)
        s = lax.dot_general(q_ref[:, sl], k_ref[:, sl], (((1,), (1,)), ((), ())),
                            preferred_element_type=F32) + bias_ref[g]
        o = _softmax_pv(_lane_tiles(s), [v_ref[:, sl]])
        o_ref[:, sl] = o.astype(o_ref.dtype)


def band_attention_step(qkv, k_all, v_all, bias, *, nb, t, heads, dh):
    d = heads * dh
    group = min(heads, 8)
    ng = heads // group
    gw = group * dh
    lk = k_all.shape[0] // nb
    return pl.pallas_call(
        functools.partial(_band_step_kernel, group=group, dh=dh),
        grid=(ng, nb),
        in_specs=[pl.BlockSpec((t, gw), lambda hg, b: (b, hg)),
                  pl.BlockSpec((lk, gw), lambda hg, b: (b, hg)),
                  pl.BlockSpec((lk, gw), lambda hg, b: (b, hg)),
                  pl.BlockSpec((group, t, lk), lambda hg, b: (hg, 0, 0))],
        out_specs=pl.BlockSpec((t, gw), lambda hg, b: (b, hg)),
        out_shape=jax.ShapeDtypeStruct((nb * t, d), BF16),
        compiler_params=_cparams("parallel", "parallel"),
        name="band_attention_step",
    )(qkv, k_all, v_all, bias)


def _mem_attn_kernel(q_ref, k_ref, v_ref, o_ref, *, heads, dh):
    for h in range(heads):
        sl = slice(h * dh, (h + 1) * dh)
        s = lax.dot_general(q_ref[:, sl], k_ref[:, sl], (((1,), (1,)), ((), ())),
                            preferred_element_type=F32)
        o = _softmax_pv(_lane_tiles(s), [v_ref[:, sl]])
        o_ref[:, sl] = o.astype(o_ref.dtype)


def mem_attention(q, mk, mv, *, nb, t, heads, dh):
    d = heads * dh
    ml = mk.shape[0] // nb
    bq = _blk(t, 512)
    nq = t // bq
    return pl.pallas_call(
        functools.partial(_mem_attn_kernel, heads=heads, dh=dh),
        grid=(nb, nq),
        in_specs=[pl.BlockSpec((bq, d), lambda b, i: (b * nq + i, 0)),
                  pl.BlockSpec((ml, d), lambda b, i: (b, 0)),
                  pl.BlockSpec((ml, d), lambda b, i: (b, 0))],
        out_specs=pl.BlockSpec((bq, d), lambda b, i: (b * nq + i, 0)),
        out_shape=jax.ShapeDtypeStruct((nb * t, d), BF16),
        compiler_params=_cparams("parallel", "parallel"),
        name="mem_attention",
    )(q, mk, mv)


def _mem_front_kernel(x_ref, y_ref, gp_ref, gn_ref, wq_ref, k_ref, v_ref, xo_ref, o_ref, *, heads, dh, scale):
    xn = x_ref[...] + _rms_f32(y_ref[...].astype(F32), gp_ref[...])
    xo_ref[...] = xn
    h = _rms_f32(xn, gn_ref[...]).astype(BF16)
    q = (jnp.dot(h, wq_ref[...], preferred_element_type=F32) * scale).astype(BF16)
    for hd in range(heads):
        sl = slice(hd * dh, (hd + 1) * dh)
        s = lax.dot_general(q[:, sl], k_ref[:, sl], (((1,), (1,)), ((), ())), preferred_element_type=F32)
        o_ref[:, sl] = _softmax_pv(_lane_tiles(s), [v_ref[:, sl]]).astype(o_ref.dtype)


def mem_front(x, y, g_post, g_pre, w_q, li, mk, mv, *, nb, t, heads, dh):
    m, d = x.shape
    md = heads * dh
    ml = mk.shape[0] // nb
    bq = _blk(t, MEM_ROWS)
    nq = t // bq
    row = lambda wd: pl.BlockSpec((bq, wd), lambda b, i: (b * nq + i, 0))
    vec = pl.BlockSpec((1, d), lambda b, i: (0, 0))
    mem = pl.BlockSpec((ml, md), lambda b, i: (b, 0))
    return pl.pallas_call(
        functools.partial(_mem_front_kernel, heads=heads, dh=dh, scale=dh ** -0.5 * LOG2E),
        grid=(nb, nq),
        in_specs=[row(d), row(d), vec, vec, pl.BlockSpec((None, d, md), lambda b, i: (li, 0, 0)), mem, mem],
        out_specs=[row(d), row(md)],
        out_shape=[jax.ShapeDtypeStruct((m, d), F32), jax.ShapeDtypeStruct((m, md), BF16)],
        compiler_params=_cparams("parallel", "parallel"),
        name="mem_front",
    )(x, y, g_post.reshape(1, d), g_pre.reshape(1, d), w_q, mk, mv)


def _mem_back_kernel(o_ref, wo_ref, x_ref, gp_ref, gn_ref, xo_ref, h_ref):
    y = jnp.dot(o_ref[...], wo_ref[...], preferred_element_type=F32)
    xn = x_ref[...] + _rms_f32(y, gp_ref[...])
    xo_ref[...] = xn
    h_ref[...] = _rms_f32(xn, gn_ref[...]).astype(h_ref.dtype)


def mem_back(o, w_o, li, x, g_post, g_next):
    m, d = x.shape
    md = o.shape[1]
    bm = _blk(m, MEM_ROWS)
    row = lambda wd: pl.BlockSpec((bm, wd), lambda i: (i, 0))
    vec = pl.BlockSpec((1, d), lambda i: (0, 0))
    return pl.pallas_call(
        _mem_back_kernel,
        grid=(m // bm,),
        in_specs=[row(md), pl.BlockSpec((None, md, d), lambda i: (li, 0, 0)), row(d), vec, vec],
        out_specs=[row(d), row(d)],
        out_shape=[jax.ShapeDtypeStruct((m, d), F32), jax.ShapeDtypeStruct((m, d), BF16)],
        compiler_params=_cparams("parallel"),
        name="mem_back",
    )(o, w_o, x, g_post.reshape(1, d), g_next.reshape(1, d))


def _cast_kernel(x_ref, o_ref):
    o_ref[...] = x_ref[...].astype(o_ref.dtype)


def cast_bf16(w, *, col_blk=0, cols=None):
    nl, r, c = w.shape
    cols = c if cols is None else cols
    assert cols == c or cols % LANE == 0
    br = _blk(r, 1 << max(3, (CAST_BLOCK_BYTES // (4 * cols)).bit_length() - 1))
    return pl.pallas_call(
        _cast_kernel,
        grid=(nl, r // br),
        in_specs=[pl.BlockSpec((None, br, cols), lambda l, i: (l, i, col_blk))],
        out_specs=pl.BlockSpec((None, br, cols), lambda l, i: (l, i, 0)),
        out_shape=jax.ShapeDtypeStruct((nl, r, cols), BF16),
        compiler_params=_cparams("parallel", "parallel"),
        name="cast_bf16",
    )(w)


def _prep_weights(W, dims):
    heads, nope, rope, dv, kvl = dims["mla_heads"], dims["nope"], dims["rope"], dims["dv"], dims["kvl"]
    ch, ql = dims["ch"], dims["ql"]
    half = rope // 2
    P = {}
    w_in = W["w_in_ab"]
    s2 = 2 * ch + ql
    P["w_in_main"] = cast_bf16(w_in, col_blk=0, cols=s2)
    kpe0 = s2 + kvl
    P["w_in_kv"] = jnp.concatenate(
        [w_in[:, :, s2:], w_in[:, :, kpe0 + half:kpe0 + rope], w_in[:, :, kpe0:kpe0 + half]],
        axis=-1).astype(BF16)
    ne = w_in.shape[0]
    wq = W["w_uq"].reshape(ne, ql, heads, nope + rope)
    pe = wq[..., nope:]
    pe_sw = jnp.concatenate([pe[..., half:], pe[..., :half]], axis=-1)
    P["w_uq"] = jnp.concatenate([wq[..., :nope], pe, pe_sw], axis=-1).reshape(
        ne, ql, heads * MXU_DIM).astype(BF16)
    wkv = W["w_ukv"].reshape(ne, kvl, heads, nope + dv)
    kslot = jnp.pad(wkv[..., :nope], ((0, 0), (0, LANE), (0, 0), (0, MXU_DIM - nope)))
    eye = jnp.zeros((LANE, MXU_DIM), F32).at[jnp.arange(rope), nope + jnp.arange(rope)].set(1.0)
    kslot = kslot + jnp.pad(eye, ((kvl, 0), (0, 0)))[None, :, None, :]
    vpart = jnp.pad(wkv[..., nope:], ((0, 0), (0, LANE), (0, 0), (0, 0)))
    P["w_ukv"] = jnp.concatenate(
        [kslot.reshape(ne, kvl + LANE, heads * MXU_DIM), vpart.reshape(ne, kvl + LANE, heads * dv)],
        axis=-1).astype(BF16)
    for name in ("w_out_ab", "w_qkv_c", "w_out_c", "w_q_mem", "w_kv_mem", "w_o_mem", "w_down"):
        P[name] = cast_bf16(W[name])
    ff = W["w_down"].shape[1]
    P["w_gate"] = cast_bf16(W["w_gate_up"], col_blk=0, cols=ff)
    P["w_up"] = cast_bf16(W["w_gate_up"], col_blk=1, cols=ff)
    P["ffp"] = -(-ff // FF_BLOCK) * FF_BLOCK
    return P


def _rope_tabs(pos, rope):
    half = rope // 2
    inv = ROPE_THETA ** (-jnp.arange(0, rope, 2, dtype=F32) / rope)
    ang = pos.astype(F32)[:, None] * inv[None, :]
    cos, sin = jnp.cos(ang), jnp.sin(ang)
    zero = jnp.zeros((pos.shape[0], LANE - rope), F32)
    assert 2 * rope == LANE and half * 2 == rope
    return jnp.concatenate([cos, cos, zero], axis=1), jnp.concatenate([-sin, sin, zero], axis=1)


def _trunk(x3, pos0, mem, mem_k_c, mem_v_c, lat_c, kpe_c, conv_c, bk_c, bv_c, W, P, dims, bias_full):
    prompt = mem is not None
    nb, t, d = x3.shape
    m = nb * t
    depth = W["norm_mix_pre"].shape[0]
    ch, ql, kvl, rope = dims["ch"], dims["ql"], dims["kvl"], dims["rope"]
    mh, nope, dv = dims["mla_heads"], dims["nope"], dims["dv"]
    cheads, cdh = dims["c_heads"], dims["c_dh"]
    memh, memdh = dims["mem_heads"], dims["mem_dh"]
    width = W["conv_w"].shape[1]
    keep = width - 1
    halo = 32
    assert keep <= halo and t % halo == 0 and (2 * ch) % ql == 0
    band_keep = BAND_CHUNKS * CHUNK

    x = x3.reshape(m, d)
    cc, ss = _rope_tabs(pos0 + jnp.arange(t, dtype=jnp.int32), rope)
    q_scale = (nope + rope) ** -0.5 * LOG2E
    lats, kpes, convs, bks, bvs, mks, mvs = [], [], [], [], [], [], []

    def tail_rows(h, rows):
        return h.reshape(nb, t, d)[:, t - rows:].reshape(nb * rows, d)

    h = rmsnorm(x, W["norm_mix_pre"][0])
    for l in range(depth):
        if l % 2 == 0:
            e = l // 2
            z_main = matmul(h, P["w_in_main"], e, BF16)
            z_kv = matmul(h, P["w_in_kv"], e, F32)
            if prompt:
                hist = jnp.zeros((nb, halo, ch), F32)
            else:
                hist = jnp.pad(conv_c[e], ((0, 0), (halo - keep, 0), (0, 0)))
            ya = conv_module(z_main, hist, W["conv_w"][e], W["conv_b"][e], W["conv_ln_g"][e],
                             W["conv_ln_b"][e], nb=nb, t=t, ch=ch)
            z_tail = matmul(tail_rows(h, halo), P["w_in_main"], e, F32, n=2 * ch)
            convs.append(glu_tail(z_tail, nb=nb, rows=halo, ch=ch, keep=keep))
            lat, kpe, a_kv = mla_kv_prep(z_kv, W["kv_norm_g"][e], cc, ss, t=t)
            lats.append(lat.reshape(nb, t, kvl))
            kpes.append(kpe.reshape(nb, t, rope))
            q = mla_q_proj(z_main, (2 * ch) // ql, W["q_norm_g"][e], P["w_uq"][e], cc, ss, t=t,
                           heads=mh, scale=q_scale)
            if prompt:
                lk, kv_len, mode = t, t, "block_causal"
                a_all = a_kv
            else:
                past = lat_c.shape[2]
                kv_len = past + t
                assert (kv_len - 1) // CHUNK <= (past // CHUNK), "decode keys must all be visible"
                lk = -(-kv_len // (3 * LANE)) * (3 * LANE)
                a_hist = jnp.concatenate(
                    [lat_c[e], kpe_c[e], jnp.zeros((nb, past, LANE - rope), F32)], axis=-1).astype(BF16)
                a_all = jnp.concatenate(
                    [a_hist, a_kv.reshape(nb, t, kvl + LANE),
                     jnp.zeros((nb, lk - kv_len, kvl + LANE), BF16)], axis=1).reshape(nb * lk, kvl + LANE)
                mode = "all_visible"
            kv = matmul(a_all, P["w_ukv"], e, BF16)
            yb = mla_attention(q, kv, nb=nb, tq=t, lk=lk, kv_len=kv_len, heads=mh, dv=dv, mode=mode)
            y = matmul_cat2(ya, yb, P["w_out_ab"], e, Y_DTYPE)
        else:
            o = l // 2
            dc = cheads * cdh
            qkv = matmul(h, P["w_qkv_c"], o, BF16, scale=cdh ** -0.5 * LOG2E, scale_cols=dc)
            if prompt:
                yc = band_attention_prompt(qkv, bias_full[o], nb=nb, t=t, heads=cheads, dh=cdh)
                rows = min(band_keep, t)
            else:
                k_new = qkv[:, dc:2 * dc].reshape(nb, t, dc)
                v_new = qkv[:, 2 * dc:].reshape(nb, t, dc)
                lh = bk_c.shape[2]
                assert lh == band_keep and t == CHUNK and pos0 % CHUNK == 0
                lkp = -(-(lh + t) // LANE) * LANE
                zpad = jnp.zeros((nb, lkp - lh - t, dc), BF16)
                k_all = jnp.concatenate([bk_c[o].reshape(nb, lh, dc).astype(BF16), k_new, zpad], axis=1)
                v_all = jnp.concatenate([bv_c[o].reshape(nb, lh, dc).astype(BF16), v_new, zpad], axis=1)
                bias_s = jnp.pad(bias_full[o][:, :t, :lh + t], ((0, 0), (0, 0), (0, lkp - lh - t)),
                                 constant_values=NEG)
                yc = band_attention_step(qkv, k_all.reshape(nb * lkp, dc), v_all.reshape(nb * lkp, dc),
                                         bias_s, nb=nb, t=t, heads=cheads, dh=cdh)
                rows = t
            kv_tail = matmul(tail_rows(h, rows), P["w_qkv_c"], o, F32, n_off=dc, n=2 * dc)
            bks.append(kv_tail[:, :dc].reshape(nb, rows, cheads, cdh))
            bvs.append(kv_tail[:, dc:].reshape(nb, rows, cheads, cdh))
            y = matmul(yc, P["w_out_c"], o, Y_DTYPE)
        md = memh * memdh
        if prompt:
            ml = mem.shape[1]
            hm = rmsnorm(mem.reshape(nb * ml, d), W["norm_memkv"][l])
            mkv = matmul(hm, P["w_kv_mem"], l, F32)
            mk32, mv32 = mkv[:, :md], mkv[:, md:]
            mks.append(mk32.reshape(nb, ml, memh, memdh))
            mvs.append(mv32.reshape(nb, ml, memh, memdh))
        else:
            ml = mem_k_c.shape[2]
            mk32 = mem_k_c[l].reshape(nb * ml, md)
            mv32 = mem_v_c[l].reshape(nb * ml, md)
        x, om = mem_front(x, y, W["norm_mix_post"][l], W["norm_mem_pre"][l], P["w_q_mem"], l,
                          mk32.astype(BF16), mv32.astype(BF16), nb=nb, t=t, heads=memh, dh=memdh)
        x, h = mem_back(om, P["w_o_mem"], l, x, W["norm_mem_post"][l], W["norm_ffn_pre"][l])

        act = swiglu_up(h, P["w_gate"], P["w_up"], l, ffp=P["ffp"])
        y = matmul_ktiled(act, P["w_down"], l, Y_DTYPE, bk=P["ffp"] // 4)
        g_next = W["norm_mix_pre"][l + 1] if l + 1 < depth else None
        x, h = resnorm(x, y, W["norm_ffn_post"][l], g_next)

    st = lambda xs: jnp.stack(xs, 0)
    return (x.reshape(nb, t, d), st(lats), st(kpes), st(convs), st(bks), st(bvs),
            st(mks) if prompt else None, st(mvs) if prompt else None)


def kernel(x_prompt, x_sample, cache_mla_latent, cache_mla_kpe, state_conv, cache_band_k,
           cache_band_v, cache_mem_k, cache_mem_v, mem_prompt, norm_mix_pre, norm_mix_post,
           norm_mem_pre, norm_mem_post, norm_memkv, norm_ffn_pre, norm_ffn_post, w_in_ab,
           conv_w, conv_b, conv_ln_g, conv_ln_b, q_norm_g, w_uq, kv_norm_g, w_ukv, w_out_ab,
           w_qkv_c, rel_bias_c, w_out_c, w_q_mem, w_kv_mem, w_o_mem, w_gate_up, w_down):
    W = dict(norm_mix_pre=norm_mix_pre, norm_mix_post=norm_mix_post, norm_mem_pre=norm_mem_pre,
             norm_mem_post=norm_mem_post, norm_memkv=norm_memkv, norm_ffn_pre=norm_ffn_pre,
             norm_ffn_post=norm_ffn_post, w_in_ab=w_in_ab, conv_w=conv_w, conv_b=conv_b,
             conv_ln_g=conv_ln_g, conv_ln_b=conv_ln_b, q_norm_g=q_norm_g, w_uq=w_uq,
             kv_norm_g=kv_norm_g, w_ukv=w_ukv, w_out_ab=w_out_ab, w_qkv_c=w_qkv_c,
             rel_bias_c=rel_bias_c, w_out_c=w_out_c, w_q_mem=w_q_mem, w_kv_mem=w_kv_mem,
             w_o_mem=w_o_mem, w_gate_up=w_gate_up, w_down=w_down)
    ch = conv_w.shape[2]
    ql = q_norm_g.shape[1]
    kvl = kv_norm_g.shape[1]
    rope = cache_mla_kpe.shape[3]
    c_heads, c_dh = cache_band_k.shape[3], cache_band_k.shape[4]
    mem_heads, mem_dh = cache_mem_k.shape[3], cache_mem_k.shape[4]
    a_cols, b_cols, v_cols = w_uq.shape[2], w_ukv.shape[2], w_out_ab.shape[1] - ch
    mla_heads = (a_cols - b_cols + v_cols) // rope
    dims = dict(ch=ch, ql=ql, kvl=kvl, rope=rope, mla_heads=mla_heads,
                nope=(b_cols - v_cols) // mla_heads, dv=v_cols // mla_heads,
                c_heads=c_heads, c_dh=c_dh, mem_heads=mem_heads, mem_dh=mem_dh)
    assert dims["nope"] == LANE and dims["dv"] == LANE and rope == LANE // 2

    P = _prep_weights(W, dims)
    bias_full = jnp.stack([jnp.transpose(band_bias(rel_bias_c[o]), (1, 0, 2))
                           for o in range(rel_bias_c.shape[0])], 0)

    past = cache_mla_latent.shape[2]
    y_p, lat_p, kpe_p, conv_p, bk_p, bv_p, mk_p, mv_p = _trunk(
        x_prompt, 0, mem_prompt, None, None, None, None, None, None, None, W, P, dims, bias_full)
    y_s, lat_s, kpe_s, conv_s, bk_s, bv_s, _, _ = _trunk(
        x_sample, past, None, cache_mem_k, cache_mem_v, cache_mla_latent, cache_mla_kpe,
        state_conv, cache_band_k, cache_band_v, W, P, dims, bias_full)
    return (y_p, y_s, lat_p, kpe_p, conv_p, bk_p, bv_p, mk_p, mv_p, lat_s, kpe_s, conv_s, bk_s, bv_s)
```

```python
import functools

import jax
import jax.numpy as jnp
from jax import lax
from jax.experimental import pallas as pl
from jax.experimental.pallas import tpu as pltpu

F32 = jnp.float32
BF16 = jnp.bfloat16
Y_DTYPE = BF16

EPS = 1e-6
NEG = -1e30
CHUNK = 64
BAND_CHUNKS = 8
ROPE_THETA = 10000.0
LOG2E = 1.4426950408889634

LANE = 128
SUBLANES = 8
MXU_DIM = 256
VMEM_LIMIT_BYTES = 56 * 1024 * 1024
FF_BLOCK = 512
CAST_BLOCK_BYTES = 8 * 1024 * 1024
MEM_ROWS = 256

BAND_QB = 4 * CHUNK
BAND_KB = BAND_CHUNKS * CHUNK + BAND_QB


def _cparams(*sem):
    return pltpu.CompilerParams(dimension_semantics=sem, vmem_limit_bytes=VMEM_LIMIT_BYTES)


def _blk(n, pref):
    if n <= pref:
        return n
    b = pref
    while n % b:
        b //= 2
    assert b >= 8, (n, pref)
    return b


def _rms_f32(x, g):
    return x * lax.rsqrt(jnp.mean(x * x, axis=-1, keepdims=True) + EPS) * g


def _rmsnorm_kernel(x_ref, g_ref, o_ref):
    o_ref[...] = _rms_f32(x_ref[...].astype(F32), g_ref[...]).astype(o_ref.dtype)


def rmsnorm(x, g, out_dtype=BF16):
    m, d = x.shape
    bm = _blk(m, 256)
    return pl.pallas_call(
        _rmsnorm_kernel,
        grid=(m // bm,),
        in_specs=[pl.BlockSpec((bm, d), lambda i: (i, 0)), pl.BlockSpec((1, d), lambda i: (0, 0))],
        out_specs=pl.BlockSpec((bm, d), lambda i: (i, 0)),
        out_shape=jax.ShapeDtypeStruct((m, d), out_dtype),
        compiler_params=_cparams("parallel"),
        name="rmsnorm",
    )(x, g.reshape(1, d))


def _resnorm_kernel(x_ref, y_ref, gp_ref, gn_ref, xo_ref, h_ref):
    xn = x_ref[...] + _rms_f32(y_ref[...].astype(F32), gp_ref[...])
    xo_ref[...] = xn
    h_ref[...] = _rms_f32(xn, gn_ref[...]).astype(h_ref.dtype)


def _resnorm_last_kernel(x_ref, y_ref, gp_ref, xo_ref):
    xo_ref[...] = x_ref[...] + _rms_f32(y_ref[...].astype(F32), gp_ref[...])


def resnorm(x, y, g_post, g_next):
    m, d = x.shape
    bm = _blk(m, 256)
    row = pl.BlockSpec((bm, d), lambda i: (i, 0))
    vec = pl.BlockSpec((1, d), lambda i: (0, 0))
    if g_next is None:
        return pl.pallas_call(
            _resnorm_last_kernel, grid=(m // bm,), in_specs=[row, row, vec], out_specs=row,
            out_shape=jax.ShapeDtypeStruct((m, d), F32), compiler_params=_cparams("parallel"),
            name="resnorm_last",
        )(x, y, g_post.reshape(1, d)), None
    return pl.pallas_call(
        _resnorm_kernel, grid=(m // bm,), in_specs=[row, row, vec, vec], out_specs=[row, row],
        out_shape=[jax.ShapeDtypeStruct((m, d), F32), jax.ShapeDtypeStruct((m, d), BF16)],
        compiler_params=_cparams("parallel"), name="resnorm",
    )(x, y, g_post.reshape(1, d), g_next.reshape(1, d))


def _mm_kernel(a_ref, w_ref, o_ref, *, scale, scale_blocks):
    acc = jnp.dot(a_ref[...], w_ref[...], preferred_element_type=F32)
    if scale is not None:
        sc = scale if scale_blocks is None else jnp.where(pl.program_id(1) < scale_blocks, scale, 1.0)
        acc = acc * sc
    o_ref[...] = acc.astype(o_ref.dtype)


def matmul(a, w, li, out_dtype, *, n_off=0, n=None, scale=None, scale_cols=None, bm_pref=1024, bn_pref=1024):
    m, k = a.shape
    n = w.shape[2] - n_off if n is None else n
    bm = _blk(m, bm_pref)
    bn = _blk(n, bn_pref)
    while n_off % bn:
        bn //= 2
    assert bn % LANE == 0 or bn == w.shape[2]
    joff = n_off // bn
    scale_blocks = None
    if scale_cols is not None:
        assert scale_cols % bn == 0
        scale_blocks = scale_cols // bn
    return pl.pallas_call(
        functools.partial(_mm_kernel, scale=scale, scale_blocks=scale_blocks),
        grid=(m // bm, n // bn),
        in_specs=[pl.BlockSpec((bm, k), lambda i, j: (i, 0)),
                  pl.BlockSpec((None, k, bn), lambda i, j: (li, 0, j + joff))],
        out_specs=pl.BlockSpec((bm, bn), lambda i, j: (i, j)),
        out_shape=jax.ShapeDtypeStruct((m, n), out_dtype),
        compiler_params=_cparams("parallel", "parallel"),
        name="matmul",
    )(a, w)


def _mm2_kernel(a1_ref, a2_ref, w1_ref, w2_ref, o_ref):
    acc = jnp.dot(a1_ref[...], w1_ref[...], preferred_element_type=F32)
    acc = acc + jnp.dot(a2_ref[...], w2_ref[...], preferred_element_type=F32)
    o_ref[...] = acc.astype(o_ref.dtype)


def matmul_cat2(a1, a2, w, li, out_dtype):
    m, k1 = a1.shape
    assert a2.shape == (m, k1) and w.shape[1] == 2 * k1
    n = w.shape[2]
    bm = _blk(m, 1024)
    bn = _blk(n, 1024)
    return pl.pallas_call(
        _mm2_kernel,
        grid=(m // bm, n // bn),
        in_specs=[pl.BlockSpec((bm, k1), lambda i, j: (i, 0)),
                  pl.BlockSpec((bm, k1), lambda i, j: (i, 0)),
                  pl.BlockSpec((None, k1, bn), lambda i, j: (li, 0, j)),
                  pl.BlockSpec((None, k1, bn), lambda i, j: (li, 1, j))],
        out_specs=pl.BlockSpec((bm, bn), lambda i, j: (i, j)),
        out_shape=jax.ShapeDtypeStruct((m, n), out_dtype),
        compiler_params=_cparams("parallel", "parallel"),
        name="matmul_cat2",
    )(a1, a2, w, w)


def _mm_acc_kernel(a_ref, w_ref, o_ref, acc_ref, *, nk, bk, k_valid):
    kk = pl.program_id(2)

    def last_w():
        w = w_ref[...]
        if k_valid < nk * bk:
            row = lax.broadcasted_iota(jnp.int32, w.shape, 0)
            w = jnp.where(row < k_valid - (nk - 1) * bk, w, jnp.zeros_like(w))
        return w

    if nk == 1:
        o_ref[...] = jnp.dot(a_ref[...], last_w(), preferred_element_type=F32).astype(o_ref.dtype)
        return

    @pl.when(kk == 0)
    def _():
        acc_ref[...] = jnp.dot(a_ref[...], w_ref[...], preferred_element_type=F32)

    @pl.when((kk > 0) & (kk < nk - 1))
    def _():
        acc_ref[...] += jnp.dot(a_ref[...], w_ref[...], preferred_element_type=F32)

    @pl.when(kk == nk - 1)
    def _():
        o_ref[...] = (acc_ref[...] + jnp.dot(a_ref[...], last_w(), preferred_element_type=F32)).astype(o_ref.dtype)


def matmul_ktiled(a, w, li, out_dtype, *, bk):
    m, k = a.shape
    _, k_valid, n = w.shape
    bm = _blk(m, 1024)
    bn = _blk(n, 1024)
    assert k % bk == 0 and k - bk < k_valid <= k
    nk = k // bk
    return pl.pallas_call(
        functools.partial(_mm_acc_kernel, nk=nk, bk=bk, k_valid=k_valid),
        grid=(m // bm, n // bn, nk),
        in_specs=[pl.BlockSpec((bm, bk), lambda i, j, kk: (i, kk)),
                  pl.BlockSpec((None, bk, bn), lambda i, j, kk: (li, kk, j))],
        out_specs=pl.BlockSpec((bm, bn), lambda i, j, kk: (i, j)),
        out_shape=jax.ShapeDtypeStruct((m, n), out_dtype),
        scratch_shapes=[pltpu.VMEM((bm, bn), F32)],
        compiler_params=_cparams("parallel", "parallel", "arbitrary"),
        name="matmul_ktiled",
    )(a, w)


def _swiglu_kernel(a_ref, wg_ref, wu_ref, o_ref, *, bn, ff):
    a = a_ref[...]
    g = jnp.dot(a, wg_ref[...], preferred_element_type=F32)
    u = jnp.dot(a, wu_ref[...], preferred_element_type=F32)
    act = g * jax.nn.sigmoid(g) * u
    col = pl.program_id(1) * bn + lax.broadcasted_iota(jnp.int32, (1, bn), 1)
    o_ref[...] = jnp.where(col < ff, act, 0.0).astype(o_ref.dtype)


def swiglu_up(h, w_gate, w_up, li, *, ffp):
    m, k = h.shape
    ff = w_gate.shape[2]
    bm = _blk(m, 1024)
    bn = FF_BLOCK
    assert ffp % bn == 0 and ffp - bn < ff <= ffp
    return pl.pallas_call(
        functools.partial(_swiglu_kernel, bn=bn, ff=ff),
        grid=(m // bm, ffp // bn),
        in_specs=[pl.BlockSpec((bm, k), lambda i, j: (i, 0)),
                  pl.BlockSpec((None, k, bn), lambda i, j: (li, 0, j)),
                  pl.BlockSpec((None, k, bn), lambda i, j: (li, 0, j))],
        out_specs=pl.BlockSpec((bm, bn), lambda i, j: (i, j)),
        out_shape=jax.ShapeDtypeStruct((m, ffp), BF16),
        compiler_params=_cparams("parallel", "parallel"),
        name="swiglu_up",
    )(h, w_gate, w_up)


def _conv_kernel(a_ref, b_ref, ah_ref, bh_ref, hist_ref, cw_ref, cb_ref, lg_ref, lb_ref, o_ref,
                 buf_ref, y_ref, *, tb, width, halo, tt):
    t_blk = pl.program_id(1)
    a = a_ref[...].astype(F32)
    u_cur = a * jax.nn.sigmoid(b_ref[...].astype(F32))
    @pl.when(t_blk == 0)
    def _():
        buf_ref[0:halo, :] = hist_ref[...]

    @pl.when(t_blk > 0)
    def _():
        ah = ah_ref[...].astype(F32)
        buf_ref[0:halo, :] = ah * jax.nn.sigmoid(bh_ref[...].astype(F32))

    buf_ref[halo:halo + tb, :] = u_cur
    ch = a.shape[1]
    base = halo - (width - 1)

    sub = SUBLANES

    def lane_tile(ct, carry):
        c0 = pl.multiple_of(ct * LANE, LANE)
        bias = cb_ref[:, pl.ds(c0, LANE)]
        for t0 in range(0, tb, tt):
            acc = jnp.broadcast_to(bias, (tt, LANE)).reshape(tt // sub, sub, LANE)
            for ph in range(sub):
                taps = [w for w in range(width) if (base + w) % sub == ph]
                if not taps:
                    continue
                span = tt + taps[-1] - taps[0]
                blk = buf_ref[pl.ds(base + t0 + taps[0], span), pl.ds(c0, LANE)].reshape(span // sub, sub, LANE)
                for w in taps:
                    off = (w - taps[0]) // sub
                    acc = acc + cw_ref[pl.ds(w * sub, sub), pl.ds(c0, LANE)][None] * blk[off:off + tt // sub]
            y_ref[pl.ds(t0, tt), pl.ds(c0, LANE)] = acc.reshape(tt, LANE)
        return carry

    lax.fori_loop(0, ch // LANE, lane_tile, 0)
    y = y_ref[...]
    mu = jnp.mean(y, axis=-1, keepdims=True)
    yc = y - mu
    var = jnp.mean(yc * yc, axis=-1, keepdims=True)
    z = yc * lax.rsqrt(var + EPS) * lg_ref[...] + lb_ref[...]
    o_ref[...] = (z * jax.nn.sigmoid(z)).astype(o_ref.dtype)


def conv_module(z_main, hist, conv_w, conv_b, ln_g, ln_b, *, nb, t, ch):
    width = conv_w.shape[0]
    halo = hist.shape[1]
    tb = _blk(t, 256)
    tt = _blk(tb, 64)
    nt = t // tb
    hb = tb // halo
    cw = jnp.repeat(conv_w, SUBLANES, axis=0)
    row = lambda c: pl.BlockSpec((tb, ch), lambda b, i: (b * nt + i, c))
    prev = lambda c: pl.BlockSpec((halo, ch), lambda b, i: (jnp.maximum((b * nt + i) * hb - 1, 0), c))
    vec = pl.BlockSpec((1, ch), lambda b, i: (0, 0))
    return pl.pallas_call(
        functools.partial(_conv_kernel, tb=tb, width=width, halo=halo, tt=tt),
        grid=(nb, nt),
        in_specs=[row(0), row(1), prev(0), prev(1),
                  pl.BlockSpec((None, halo, ch), lambda b, i: (b, 0, 0)),
                  pl.BlockSpec((width * SUBLANES, ch), lambda b, i: (0, 0)), vec, vec, vec],
        out_specs=pl.BlockSpec((tb, ch), lambda b, i: (b * nt + i, 0)),
        out_shape=jax.ShapeDtypeStruct((nb * t, ch), BF16),
        scratch_shapes=[pltpu.VMEM((halo + tb, ch), F32), pltpu.VMEM((tb, ch), F32)],
        compiler_params=_cparams("parallel", "parallel"),
        name="conv_module",
    )(z_main, z_main, z_main, z_main, hist, cw, conv_b.reshape(1, ch), ln_g.reshape(1, ch),
      ln_b.reshape(1, ch))


def _glu_tail_kernel(z_ref, o_ref, *, ch, keep):
    z = z_ref[...]
    u = z[:, :ch] * jax.nn.sigmoid(z[:, ch:])
    o_ref[...] = u[u.shape[0] - keep:, :]


def glu_tail(z_tail, *, nb, rows, ch, keep):
    return pl.pallas_call(
        functools.partial(_glu_tail_kernel, ch=ch, keep=keep),
        grid=(nb,),
        in_specs=[pl.BlockSpec((rows, 2 * ch), lambda b: (b, 0))],
        out_specs=pl.BlockSpec((None, keep, ch), lambda b: (b, 0, 0)),
        out_shape=jax.ShapeDtypeStruct((nb, keep, ch), F32),
        compiler_params=_cparams("parallel"),
        name="glu_tail",
    )(z_tail)


def _rope_pairs(v, cc, ss):
    return v * cc + pltpu.roll(v, LANE // 2, axis=1) * ss


def _mla_kv_prep_kernel(z_ref, g_ref, cc_ref, ss_ref, lat_ref, kpe_ref, a_ref, *, kvl):
    z = z_ref[...]
    lat = _rms_f32(z[:, :kvl], g_ref[...])
    rot = _rope_pairs(z[:, kvl:kvl + LANE], cc_ref[...], ss_ref[...])
    lat_ref[...] = lat
    kpe_ref[...] = rot[:, :LANE // 2]
    a_ref[:, :kvl] = lat.astype(a_ref.dtype)
    a_ref[:, kvl:] = rot.astype(a_ref.dtype)


def mla_kv_prep(z_kv, g, cc, ss, *, t):
    m, wdt = z_kv.shape
    kvl = wdt - LANE
    bm = _blk(t, 256)
    nt = t // bm
    row = lambda wd: pl.BlockSpec((bm, wd), lambda i: (i, 0))
    tab = pl.BlockSpec((bm, LANE), lambda i: (i % nt, 0))
    return pl.pallas_call(
        functools.partial(_mla_kv_prep_kernel, kvl=kvl),
        grid=(m // bm,),
        in_specs=[row(wdt), pl.BlockSpec((1, kvl), lambda i: (0, 0)), tab, tab],
        out_specs=[row(kvl), row(LANE // 2), row(wdt)],
        out_shape=[jax.ShapeDtypeStruct((m, kvl), F32), jax.ShapeDtypeStruct((m, LANE // 2), F32),
                   jax.ShapeDtypeStruct((m, wdt), BF16)],
        compiler_params=_cparams("parallel"),
        name="mla_kv_prep",
    )(z_kv, g.reshape(1, kvl), cc, ss)


def _mla_q_kernel(z_ref, g_ref, w_ref, cc_ref, ss_ref, o_ref, *, heads, scale):
    a = _rms_f32(z_ref[...].astype(F32), g_ref[...]).astype(BF16)
    acc = jnp.dot(a, w_ref[...], preferred_element_type=F32)
    cc = cc_ref[...]
    ss = ss_ref[...]
    for h in range(heads):
        c0 = h * MXU_DIM
        o_ref[:, c0:c0 + LANE] = (acc[:, c0:c0 + LANE] * scale).astype(o_ref.dtype)
        rot = _rope_pairs(acc[:, c0 + LANE:c0 + MXU_DIM], cc, ss)
        o_ref[:, c0 + LANE:c0 + MXU_DIM] = (rot * scale).astype(o_ref.dtype)


def mla_q_proj(z_main, col_blk, g, w_uq_ext, cc, ss, *, t, heads, scale):
    m = z_main.shape[0]
    ql, n = w_uq_ext.shape
    bm = _blk(t, 512)
    nt = t // bm
    tab = pl.BlockSpec((bm, LANE), lambda i: (i % nt, 0))
    return pl.pallas_call(
        functools.partial(_mla_q_kernel, heads=heads, scale=scale),
        grid=(m // bm,),
        in_specs=[pl.BlockSpec((bm, ql), lambda i: (i, col_blk)),
                  pl.BlockSpec((1, ql), lambda i: (0, 0)),
                  pl.BlockSpec((ql, n), lambda i: (0, 0)), tab, tab],
        out_specs=pl.BlockSpec((bm, n), lambda i: (i, 0)),
        out_shape=jax.ShapeDtypeStruct((m, n), BF16),
        compiler_params=_cparams("parallel"),
        name="mla_q_proj",
    )(z_main, g.reshape(1, ql), w_uq_ext, cc, ss)


def _mla_attn_kernel(q_ref, k_ref, v_ref, o_ref, m_ref, l_ref, acc_ref, *, heads, dv, bq, bk, nk,
                     mode, kv_len):
    i = pl.program_id(1)
    j = pl.program_id(2)

    @pl.when(j == 0)
    def _():
        m_ref[...] = jnp.full(m_ref.shape, NEG, F32)
        l_ref[...] = jnp.zeros(l_ref.shape, F32)
        acc_ref[...] = jnp.zeros(acc_ref.shape, F32)

    def step(mask):
        for h in range(heads):
            qh = q_ref[:, h * MXU_DIM:(h + 1) * MXU_DIM]
            kh = k_ref[:, h * MXU_DIM:(h + 1) * MXU_DIM]
            s = lax.dot_general(qh, kh, (((1,), (1,)), ((), ())), preferred_element_type=F32)
            if mask is not None:
                s = jnp.where(mask, s, NEG)
            tiles = [s[:, t * LANE:(t + 1) * LANE] for t in range(bk // LANE)]
            mx = tiles[0]
            for tl in tiles[1:]:
                mx = jnp.maximum(mx, tl)
            m_prev = m_ref[h]
            m_new = jnp.maximum(m_prev, jnp.max(mx, axis=1, keepdims=True))
            alpha = jnp.exp2(m_prev - m_new)
            ps = [jnp.exp2(tl - m_new) for tl in tiles]
            lsum = ps[0]
            for p in ps[1:]:
                lsum = lsum + p
            l_ref[h] = alpha * l_ref[h] + lsum
            pb = jnp.concatenate(ps, axis=1).astype(BF16)
            pv = jnp.dot(pb, v_ref[:, h * dv:(h + 1) * dv], preferred_element_type=F32)
            acc_ref[:, h * dv:(h + 1) * dv] = alpha * acc_ref[:, h * dv:(h + 1) * dv] + pv
            m_ref[h] = m_new

    if mode == "block_causal":
        @pl.when(j < i)
        def _():
            step(None)

        @pl.when(j == i)
        def _():
            qc = lax.broadcasted_iota(jnp.int32, (bq, bk), 0) // CHUNK
            kc = lax.broadcasted_iota(jnp.int32, (bq, bk), 1) // CHUNK
            step(kc <= qc)
    else:
        @pl.when(j < nk - 1)
        def _():
            step(None)

        @pl.when(j == nk - 1)
        def _():
            kpos = (nk - 1) * bk + lax.broadcasted_iota(jnp.int32, (bq, bk), 1)
            step(kpos < kv_len if kv_len < nk * bk else None)

    @pl.when(j == nk - 1)
    def _():
        for h in range(heads):
            l = jnp.sum(l_ref[h], axis=1, keepdims=True)
            o_ref[:, h * dv:(h + 1) * dv] = (acc_ref[:, h * dv:(h + 1) * dv] / l).astype(o_ref.dtype)


def mla_attention(q, kv, *, nb, tq, lk, kv_len, heads, dv, mode):
    kw = heads * MXU_DIM
    vw = heads * dv
    assert kw % vw == 0 and dv == LANE
    bq = _blk(tq, 512)
    if mode == "block_causal":
        assert lk == tq and kv_len == lk and bq % CHUNK == 0
        bk = bq
    else:
        bk = _blk(lk, 768)
    nq, nk = tq // bq, lk // bk
    assert kv_len > (nk - 1) * bk, "key padding must sit in the last key block"
    if mode == "block_causal":
        kidx = lambda b, i, j: (b * nk + jnp.minimum(j, i), 0)
        vidx = lambda b, i, j: (b * nk + jnp.minimum(j, i), kw // vw)
    else:
        kidx = lambda b, i, j: (b * nk + j, 0)
        vidx = lambda b, i, j: (b * nk + j, kw // vw)
    return pl.pallas_call(
        functools.partial(_mla_attn_kernel, heads=heads, dv=dv, bq=bq, bk=bk, nk=nk, mode=mode,
                          kv_len=kv_len),
        grid=(nb, nq, nk),
        in_specs=[pl.BlockSpec((bq, kw), lambda b, i, j: (b * nq + i, 0)),
                  pl.BlockSpec((bk, kw), kidx),
                  pl.BlockSpec((bk, vw), vidx)],
        out_specs=pl.BlockSpec((bq, vw), lambda b, i, j: (b * nq + i, 0)),
        out_shape=jax.ShapeDtypeStruct((nb * tq, vw), BF16),
        scratch_shapes=[pltpu.VMEM((heads, bq, LANE), F32), pltpu.VMEM((heads, bq, LANE), F32),
                        pltpu.VMEM((bq, vw), F32)],
        compiler_params=_cparams("parallel", "parallel", "arbitrary"),
        name="mla_attention",
    )(q, kv, kv)


def _band_bias_kernel(t_ref, o_ref, *, rows, max_rel, kb, past):
    t = t_ref[...]
    hi = t.astype(BF16)
    r1 = t - hi.astype(F32)
    mid = r1.astype(BF16)
    lo = (r1 - mid.astype(F32)).astype(BF16)
    nrel = t.shape[1]
    j = lax.broadcasted_iota(jnp.int32, (nrel, kb), 1)
    r = lax.broadcasted_iota(jnp.int32, (nrel, kb), 0)
    kc = lax.broadcasted_iota(jnp.int32, (t.shape[0], kb), 1) // CHUNK - past // CHUNK
    for rr in range(rows):
        i = pl.program_id(0) * rows + rr
        idx = jnp.clip(i - j + past, -max_rel, max_rel) + max_rel
        onehot = jnp.where(r == idx, 1.0, 0.0).astype(BF16)
        vals = (jnp.dot(hi, onehot, preferred_element_type=F32)
                + jnp.dot(mid, onehot, preferred_element_type=F32)
                + jnp.dot(lo, onehot, preferred_element_type=F32))
        qc = i // CHUNK
        ok = (kc <= qc) & (kc >= qc - BAND_CHUNKS)
        o_ref[rr] = jnp.where(ok, vals * LOG2E, NEG)


def band_bias(table):
    heads, nrel = table.shape
    max_rel = (nrel - 1) // 2
    nrel_p = -(-nrel // LANE) * LANE
    tp = jnp.pad(table, ((0, 0), (0, nrel_p - nrel)))
    rows = 8
    past = BAND_CHUNKS * CHUNK
    return pl.pallas_call(
        functools.partial(_band_bias_kernel, rows=rows, max_rel=max_rel, kb=BAND_KB, past=past),
        grid=(BAND_QB // rows,),
        in_specs=[pl.BlockSpec((heads, nrel_p), lambda i: (0, 0))],
        out_specs=pl.BlockSpec((rows, heads, BAND_KB), lambda i: (i, 0, 0)),
        out_shape=jax.ShapeDtypeStruct((BAND_QB, heads, BAND_KB), F32),
        compiler_params=_cparams("parallel"),
        name="band_bias",
    )(tp)


def _softmax_pv(tiles, v_parts):
    mx = tiles[0]
    for tl in tiles[1:]:
        mx = jnp.maximum(mx, tl)
    m = jnp.max(mx, axis=1, keepdims=True)
    ps = [jnp.exp2(tl - m) for tl in tiles]
    lsum = ps[0]
    for p in ps[1:]:
        lsum = lsum + p
    l = jnp.sum(lsum, axis=1, keepdims=True)
    per = len(tiles) // len(v_parts)
    o = None
    for n, vp in enumerate(v_parts):
        pb = jnp.concatenate(ps[n * per:(n + 1) * per], axis=1).astype(BF16)
        part = jnp.dot(pb, vp, preferred_element_type=F32)
        o = part if o is None else o + part
    return o / l


def _lane_tiles(s):
    return [s[:, c:c + LANE] for c in range(0, s.shape[1], LANE)]


def _band_prompt_kernel(q_ref, k0_ref, k1_ref, k2_ref, v0_ref, v1_ref, v2_ref, bias_ref, o_ref, *,
                        group, dh):
    qb = pl.program_id(2)
    pen = [jnp.where(qb < 2, NEG, 0.0), jnp.where(qb < 1, NEG, 0.0), None]
    for g in range(group):
        sl = slice(g * dh, (g + 1) * dh)
        qh = q_ref[:, sl]
        tiles = []
        for n, kr in enumerate((k0_ref, k1_ref, k2_ref)):
            part = lax.dot_general(qh, kr[:, sl], (((1,), (1,)), ((), ())), preferred_element_type=F32)
            for c in range(0, BAND_QB, LANE):
                tl = part[:, c:c + LANE] + bias_ref[g, :, n * BAND_QB + c:n * BAND_QB + c + LANE]
                tiles.append(tl if pen[n] is None else tl + pen[n])
        o = _softmax_pv(tiles, [v0_ref[:, sl], v1_ref[:, sl], v2_ref[:, sl]])
        o_ref[:, sl] = o.astype(o_ref.dtype)


def band_attention_prompt(qkv, bias, *, nb, t, heads, dh):
    d = heads * dh
    group = min(heads, 8)
    ng = heads // group
    gw = group * dh
    nq = t // BAND_QB
    assert t % BAND_QB == 0 and BAND_KB == 3 * BAND_QB

    def kspec(back, col0):
        return pl.BlockSpec((BAND_QB, gw),
                            lambda hg, b, i: (b * nq + jnp.maximum(i - back, 0), col0 + hg))

    return pl.pallas_call(
        functools.partial(_band_prompt_kernel, group=group, dh=dh),
        grid=(ng, nb, nq),
        in_specs=[pl.BlockSpec((BAND_QB, gw), lambda hg, b, i: (b * nq + i, hg)),
                  kspec(2, ng), kspec(1, ng), kspec(0, ng),
                  kspec(2, 2 * ng), kspec(1, 2 * ng), kspec(0, 2 * ng),
                  pl.BlockSpec((group, BAND_QB, BAND_KB), lambda hg, b, i: (hg, 0, 0))],
        out_specs=pl.BlockSpec((BAND_QB, gw), lambda hg, b, i: (b * nq + i, hg)),
        out_shape=jax.ShapeDtypeStruct((nb * t, d), BF16),
        compiler_params=_cparams("parallel", "parallel", "parallel"),
        name="band_attention_prompt",
    )(qkv, qkv, qkv, qkv, qkv, qkv, qkv, bias)


def _band_step_kernel(q_ref, k_ref, v_ref, bias_ref, o_ref, *, group, dh):
    for g in range(group):
        sl = slice(g * dh, (g + 1) * dh)
        s = lax.dot_general(q_ref[:, sl], k_ref[:, sl], (((1,), (1,)), ((), ())),
                            preferred_element_type=F32) + bias_ref[g]
        o = _softmax_pv(_lane_tiles(s), [v_ref[:, sl]])
        o_ref[:, sl] = o.astype(o_ref.dtype)


def band_attention_step(qkv, k_all, v_all, bias, *, nb, t, heads, dh):
    d = heads * dh
    group = min(heads, 8)
    ng = heads // group
    gw = group * dh
    lk = k_all.shape[0] // nb
    return pl.pallas_call(
        functools.partial(_band_step_kernel, group=group, dh=dh),
        grid=(ng, nb),
        in_specs=[pl.BlockSpec((t, gw), lambda hg, b: (b, hg)),
                  pl.BlockSpec((lk, gw), lambda hg, b: (b, hg)),
                  pl.BlockSpec((lk, gw), lambda hg, b: (b, hg)),
                  pl.BlockSpec((group, t, lk), lambda hg, b: (hg, 0, 0))],
        out_specs=pl.BlockSpec((t, gw), lambda hg, b: (b, hg)),
        out_shape=jax.ShapeDtypeStruct((nb * t, d), BF16),
        compiler_params=_cparams("parallel", "parallel"),
        name="band_attention_step",
    )(qkv, k_all, v_all, bias)


def _mem_attn_kernel(q_ref, k_ref, v_ref, o_ref, *, heads, dh):
    for h in range(heads):
        sl = slice(h * dh, (h + 1) * dh)
        s = lax.dot_general(q_ref[:, sl], k_ref[:, sl], (((1,), (1,)), ((), ())),
                            preferred_element_type=F32)
        o = _softmax_pv(_lane_tiles(s), [v_ref[:, sl]])
        o_ref[:, sl] = o.astype(o_ref.dtype)


def mem_attention(q, mk, mv, *, nb, t, heads, dh):
    d = heads * dh
    ml = mk.shape[0] // nb
    bq = _blk(t, 512)
    nq = t // bq
    return pl.pallas_call(
        functools.partial(_mem_attn_kernel, heads=heads, dh=dh),
        grid=(nb, nq),
        in_specs=[pl.BlockSpec((bq, d), lambda b, i: (b * nq + i, 0)),
                  pl.BlockSpec((ml, d), lambda b, i: (b, 0)),
                  pl.BlockSpec((ml, d), lambda b, i: (b, 0))],
        out_specs=pl.BlockSpec((bq, d), lambda b, i: (b * nq + i, 0)),
        out_shape=jax.ShapeDtypeStruct((nb * t, d), BF16),
        compiler_params=_cparams("parallel", "parallel"),
        name="mem_attention",
    )(q, mk, mv)


def _mem_front_kernel(x_ref, y_ref, gp_ref, gn_ref, wq_ref, k_ref, v_ref, xo_ref, o_ref, *, heads, dh, scale):
    xn = x_ref[...] + _rms_f32(y_ref[...].astype(F32), gp_ref[...])
    xo_ref[...] = xn
    h = _rms_f32(xn, gn_ref[...]).astype(BF16)
    q = (jnp.dot(h, wq_ref[...], preferred_element_type=F32) * scale).astype(BF16)
    for hd in range(heads):
        sl = slice(hd * dh, (hd + 1) * dh)
        s = lax.dot_general(q[:, sl], k_ref[:, sl], (((1,), (1,)), ((), ())), preferred_element_type=F32)
        o_ref[:, sl] = _softmax_pv(_lane_tiles(s), [v_ref[:, sl]]).astype(o_ref.dtype)


def mem_front(x, y, g_post, g_pre, w_q, li, mk, mv, *, nb, t, heads, dh):
    m, d = x.shape
    md = heads * dh
    ml = mk.shape[0] // nb
    bq = _blk(t, MEM_ROWS)
    nq = t // bq
    row = lambda wd: pl.BlockSpec((bq, wd), lambda b, i: (b * nq + i, 0))
    vec = pl.BlockSpec((1, d), lambda b, i: (0, 0))
    mem = pl.BlockSpec((ml, md), lambda b, i: (b, 0))
    return pl.pallas_call(
        functools.partial(_mem_front_kernel, heads=heads, dh=dh, scale=dh ** -0.5 * LOG2E),
        grid=(nb, nq),
        in_specs=[row(d), row(d), vec, vec, pl.BlockSpec((None, d, md), lambda b, i: (li, 0, 0)), mem, mem],
        out_specs=[row(d), row(md)],
        out_shape=[jax.ShapeDtypeStruct((m, d), F32), jax.ShapeDtypeStruct((m, md), BF16)],
        compiler_params=_cparams("parallel", "parallel"),
        name="mem_front",
    )(x, y, g_post.reshape(1, d), g_pre.reshape(1, d), w_q, mk, mv)


def _mem_back_kernel(o_ref, wo_ref, x_ref, gp_ref, gn_ref, xo_ref, h_ref):
    y = jnp.dot(o_ref[...], wo_ref[...], preferred_element_type=F32)
    xn = x_ref[...] + _rms_f32(y, gp_ref[...])
    xo_ref[...] = xn
    h_ref[...] = _rms_f32(xn, gn_ref[...]).astype(h_ref.dtype)


def mem_back(o, w_o, li, x, g_post, g_next):
    m, d = x.shape
    md = o.shape[1]
    bm = _blk(m, MEM_ROWS)
    row = lambda wd: pl.BlockSpec((bm, wd), lambda i: (i, 0))
    vec = pl.BlockSpec((1, d), lambda i: (0, 0))
    return pl.pallas_call(
        _mem_back_kernel,
        grid=(m // bm,),
        in_specs=[row(md), pl.BlockSpec((None, md, d), lambda i: (li, 0, 0)), row(d), vec, vec],
        out_specs=[row(d), row(d)],
        out_shape=[jax.ShapeDtypeStruct((m, d), F32), jax.ShapeDtypeStruct((m, d), BF16)],
        compiler_params=_cparams("parallel"),
        name="mem_back",
    )(o, w_o, x, g_post.reshape(1, d), g_next.reshape(1, d))


def _cast_kernel(x_ref, o_ref):
    o_ref[...] = x_ref[...].astype(o_ref.dtype)


def cast_bf16(w, *, col_blk=0, cols=None):
    nl, r, c = w.shape
    cols = c if cols is None else cols
    assert cols == c or cols % LANE == 0
    br = _blk(r, 1 << max(3, (CAST_BLOCK_BYTES // (4 * cols)).bit_length() - 1))
    return pl.pallas_call(
        _cast_kernel,
        grid=(nl, r // br),
        in_specs=[pl.BlockSpec((None, br, cols), lambda l, i: (l, i, col_blk))],
        out_specs=pl.BlockSpec((None, br, cols), lambda l, i: (l, i, 0)),
        out_shape=jax.ShapeDtypeStruct((nl, r, cols), BF16),
        compiler_params=_cparams("parallel", "parallel"),
        name="cast_bf16",
    )(w)


def _prep_weights(W, dims):
    heads, nope, rope, dv, kvl = dims["mla_heads"], dims["nope"], dims["rope"], dims["dv"], dims["kvl"]
    ch, ql = dims["ch"], dims["ql"]
    half = rope // 2
    P = {}
    w_in = W["w_in_ab"]
    s2 = 2 * ch + ql
    P["w_in_main"] = cast_bf16(w_in, col_blk=0, cols=s2)
    kpe0 = s2 + kvl
    P["w_in_kv"] = jnp.concatenate(
        [w_in[:, :, s2:], w_in[:, :, kpe0 + half:kpe0 + rope], w_in[:, :, kpe0:kpe0 + half]],
        axis=-1).astype(BF16)
    ne = w_in.shape[0]
    wq = W["w_uq"].reshape(ne, ql, heads, nope + rope)
    pe = wq[..., nope:]
    pe_sw = jnp.concatenate([pe[..., half:], pe[..., :half]], axis=-1)
    P["w_uq"] = jnp.concatenate([wq[..., :nope], pe, pe_sw], axis=-1).reshape(
        ne, ql, heads * MXU_DIM).astype(BF16)
    wkv = W["w_ukv"].reshape(ne, kvl, heads, nope + dv)
    kslot = jnp.pad(wkv[..., :nope], ((0, 0), (0, LANE), (0, 0), (0, MXU_DIM - nope)))
    eye = jnp.zeros((LANE, MXU_DIM), F32).at[jnp.arange(rope), nope + jnp.arange(rope)].set(1.0)
    kslot = kslot + jnp.pad(eye, ((kvl, 0), (0, 0)))[None, :, None, :]
    vpart = jnp.pad(wkv[..., nope:], ((0, 0), (0, LANE), (0, 0), (0, 0)))
    P["w_ukv"] = jnp.concatenate(
        [kslot.reshape(ne, kvl + LANE, heads * MXU_DIM), vpart.reshape(ne, kvl + LANE, heads * dv)],
        axis=-1).astype(BF16)
    for name in ("w_out_ab", "w_qkv_c", "w_out_c", "w_q_mem", "w_kv_mem", "w_o_mem", "w_down"):
        P[name] = cast_bf16(W[name])
    ff = W["w_down"].shape[1]
    P["w_gate"] = cast_bf16(W["w_gate_up"], col_blk=0, cols=ff)
    P["w_up"] = cast_bf16(W["w_gate_up"], col_blk=1, cols=ff)
    P["ffp"] = -(-ff // FF_BLOCK) * FF_BLOCK
    return P


def _rope_tabs(pos, rope):
    half = rope // 2
    inv = ROPE_THETA ** (-jnp.arange(0, rope, 2, dtype=F32) / rope)
    ang = pos.astype(F32)[:, None] * inv[None, :]
    cos, sin = jnp.cos(ang), jnp.sin(ang)
    zero = jnp.zeros((pos.shape[0], LANE - rope), F32)
    assert 2 * rope == LANE and half * 2 == rope
    return jnp.concatenate([cos, cos, zero], axis=1), jnp.concatenate([-sin, sin, zero], axis=1)


def _trunk(x3, pos0, mem, mem_k_c, mem_v_c, lat_c, kpe_c, conv_c, bk_c, bv_c, W, P, dims, bias_full):
    prompt = mem is not None
    nb, t, d = x3.shape
    m = nb * t
    depth = W["norm_mix_pre"].shape[0]
    ch, ql, kvl, rope = dims["ch"], dims["ql"], dims["kvl"], dims["rope"]
    mh, nope, dv = dims["mla_heads"], dims["nope"], dims["dv"]
    cheads, cdh = dims["c_heads"], dims["c_dh"]
    memh, memdh = dims["mem_heads"], dims["mem_dh"]
    width = W["conv_w"].shape[1]
    keep = width - 1
    halo = 32
    assert keep <= halo and t % halo == 0 and (2 * ch) % ql == 0
    band_keep = BAND_CHUNKS * CHUNK

    x = x3.reshape(m, d)
    cc, ss = _rope_tabs(pos0 + jnp.arange(t, dtype=jnp.int32), rope)
    q_scale = (nope + rope) ** -0.5 * LOG2E
    lats, kpes, convs, bks, bvs, mks, mvs = [], [], [], [], [], [], []

    def tail_rows(h, rows):
        return h.reshape(nb, t, d)[:, t - rows:].reshape(nb * rows, d)

    h = rmsnorm(x, W["norm_mix_pre"][0])
    for l in range(depth):
        if l % 2 == 0:
            e = l // 2
            z_main = matmul(h, P["w_in_main"], e, BF16)
            z_kv = matmul(h, P["w_in_kv"], e, F32)
            if prompt:
                hist = jnp.zeros((nb, halo, ch), F32)
            else:
                hist = jnp.pad(conv_c[e], ((0, 0), (halo - keep, 0), (0, 0)))
            ya = conv_module(z_main, hist, W["conv_w"][e], W["conv_b"][e], W["conv_ln_g"][e],
                             W["conv_ln_b"][e], nb=nb, t=t, ch=ch)
            z_tail = matmul(tail_rows(h, halo), P["w_in_main"], e, F32, n=2 * ch)
            convs.append(glu_tail(z_tail, nb=nb, rows=halo, ch=ch, keep=keep))
            lat, kpe, a_kv = mla_kv_prep(z_kv, W["kv_norm_g"][e], cc, ss, t=t)
            lats.append(lat.reshape(nb, t, kvl))
            kpes.append(kpe.reshape(nb, t, rope))
            q = mla_q_proj(z_main, (2 * ch) // ql, W["q_norm_g"][e], P["w_uq"][e], cc, ss, t=t,
                           heads=mh, scale=q_scale)
            if prompt:
                lk, kv_len, mode = t, t, "block_causal"
                a_all = a_kv
            else:
                past = lat_c.shape[2]
                kv_len = past + t
                assert (kv_len - 1) // CHUNK <= (past // CHUNK), "decode keys must all be visible"
                lk = -(-kv_len // (3 * LANE)) * (3 * LANE)
                a_hist = jnp.concatenate(
                    [lat_c[e], kpe_c[e], jnp.zeros((nb, past, LANE - rope), F32)], axis=-1).astype(BF16)
                a_all = jnp.concatenate(
                    [a_hist, a_kv.reshape(nb, t, kvl + LANE),
                     jnp.zeros((nb, lk - kv_len, kvl + LANE), BF16)], axis=1).reshape(nb * lk, kvl + LANE)
                mode = "all_visible"
            kv = matmul(a_all, P["w_ukv"], e, BF16)
            yb = mla_attention(q, kv, nb=nb, tq=t, lk=lk, kv_len=kv_len, heads=mh, dv=dv, mode=mode)
            y = matmul_cat2(ya, yb, P["w_out_ab"], e, Y_DTYPE)
        else:
            o = l // 2
            dc = cheads * cdh
            qkv = matmul(h, P["w_qkv_c"], o, BF16, scale=cdh ** -0.5 * LOG2E, scale_cols=dc)
            if prompt:
                yc = band_attention_prompt(qkv, bias_full[o], nb=nb, t=t, heads=cheads, dh=cdh)
                rows = min(band_keep, t)
            else:
                k_new = qkv[:, dc:2 * dc].reshape(nb, t, dc)
                v_new = qkv[:, 2 * dc:].reshape(nb, t, dc)
                lh = bk_c.shape[2]
                assert lh == band_keep and t == CHUNK and pos0 % CHUNK == 0
                lkp = -(-(lh + t) // LANE) * LANE
                zpad = jnp.zeros((nb, lkp - lh - t, dc), BF16)
                k_all = jnp.concatenate([bk_c[o].reshape(nb, lh, dc).astype(BF16), k_new, zpad], axis=1)
                v_all = jnp.concatenate([bv_c[o].reshape(nb, lh, dc).astype(BF16), v_new, zpad], axis=1)
                bias_s = jnp.pad(bias_full[o][:, :t, :lh + t], ((0, 0), (0, 0), (0, lkp - lh - t)),
                                 constant_values=NEG)
                yc = band_attention_step(qkv, k_all.reshape(nb * lkp, dc), v_all.reshape(nb * lkp, dc),
                                         bias_s, nb=nb, t=t, heads=cheads, dh=cdh)
                rows = t
            kv_tail = matmul(tail_rows(h, rows), P["w_qkv_c"], o, F32, n_off=dc, n=2 * dc)
            bks.append(kv_tail[:, :dc].reshape(nb, rows, cheads, cdh))
            bvs.append(kv_tail[:, dc:].reshape(nb, rows, cheads, cdh))
            y = matmul(yc, P["w_out_c"], o, Y_DTYPE)
        md = memh * memdh
        if prompt:
            ml = mem.shape[1]
            hm = rmsnorm(mem.reshape(nb * ml, d), W["norm_memkv"][l])
            mkv = matmul(hm, P["w_kv_mem"], l, F32)
            mk32, mv32 = mkv[:, :md], mkv[:, md:]
            mks.append(mk32.reshape(nb, ml, memh, memdh))
            mvs.append(mv32.reshape(nb, ml, memh, memdh))
        else:
            ml = mem_k_c.shape[2]
            mk32 = mem_k_c[l].reshape(nb * ml, md)
            mv32 = mem_v_c[l].reshape(nb * ml, md)
        x, om = mem_front(x, y, W["norm_mix_post"][l], W["norm_mem_pre"][l], P["w_q_mem"], l,
                          mk32.astype(BF16), mv32.astype(BF16), nb=nb, t=t, heads=memh, dh=memdh)
        x, h = mem_back(om, P["w_o_mem"], l, x, W["norm_mem_post"][l], W["norm_ffn_pre"][l])

        act = swiglu_up(h, P["w_gate"], P["w_up"], l, ffp=P["ffp"])
        y = matmul_ktiled(act, P["w_down"], l, Y_DTYPE, bk=P["ffp"] // 4)
        g_next = W["norm_mix_pre"][l + 1] if l + 1 < depth else None
        x, h = resnorm(x, y, W["norm_ffn_post"][l], g_next)

    st = lambda xs: jnp.stack(xs, 0)
    return (x.reshape(nb, t, d), st(lats), st(kpes), st(convs), st(bks), st(bvs),
            st(mks) if prompt else None, st(mvs) if prompt else None)


def kernel(x_prompt, x_sample, cache_mla_latent, cache_mla_kpe, state_conv, cache_band_k,
           cache_band_v, cache_mem_k, cache_mem_v, mem_prompt, norm_mix_pre, norm_mix_post,
           norm_mem_pre, norm_mem_post, norm_memkv, norm_ffn_pre, norm_ffn_post, w_in_ab,
           conv_w, conv_b, conv_ln_g, conv_ln_b, q_norm_g, w_uq, kv_norm_g, w_ukv, w_out_ab,
           w_qkv_c, rel_bias_c, w_out_c, w_q_mem, w_kv_mem, w_o_mem, w_gate_up, w_down):
    W = dict(norm_mix_pre=norm_mix_pre, norm_mix_post=norm_mix_post, norm_mem_pre=norm_mem_pre,
             norm_mem_post=norm_mem_post, norm_memkv=norm_memkv, norm_ffn_pre=norm_ffn_pre,
             norm_ffn_post=norm_ffn_post, w_in_ab=w_in_ab, conv_w=conv_w, conv_b=conv_b,
             conv_ln_g=conv_ln_g, conv_ln_b=conv_ln_b, q_norm_g=q_norm_g, w_uq=w_uq,
             kv_norm_g=kv_norm_g, w_ukv=w_ukv, w_out_ab=w_out_ab, w_qkv_c=w_qkv_c,
             rel_bias_c=rel_bias_c, w_out_c=w_out_c, w_q_mem=w_q_mem, w_kv_mem=w_kv_mem,
             w_o_mem=w_o_mem, w_gate_up=w_gate_up, w_down=w_down)
    ch = conv_w.shape[2]
    ql = q_norm_g.shape[1]
    kvl = kv_norm_g.shape[1]
    rope = cache_mla_kpe.shape[3]
    c_heads, c_dh = cache_band_k.shape[3], cache_band_k.shape[4]
    mem_heads, mem_dh = cache_mem_k.shape[3], cache_mem_k.shape[4]
    a_cols, b_cols, v_cols = w_uq.shape[2], w_ukv.shape[2], w_out_ab.shape[1] - ch
    mla_heads = (a_cols - b_cols + v_cols) // rope
    dims = dict(ch=ch, ql=ql, kvl=kvl, rope=rope, mla_heads=mla_heads,
                nope=(b_cols - v_cols) // mla_heads, dv=v_cols // mla_heads,
                c_heads=c_heads, c_dh=c_dh, mem_heads=mem_heads, mem_dh=mem_dh)
    assert dims["nope"] == LANE and dims["dv"] == LANE and rope == LANE // 2

    P = _prep_weights(W, dims)
    bias_full = jnp.stack([jnp.transpose(band_bias(rel_bias_c[o]), (1, 0, 2))
                           for o in range(rel_bias_c.shape[0])], 0)

    past = cache_mla_latent.shape[2]
    y_p, lat_p, kpe_p, conv_p, bk_p, bv_p, mk_p, mv_p = _trunk(
        x_prompt, 0, mem_prompt, None, None, None, None, None, None, None, W, P, dims, bias_full)
    y_s, lat_s, kpe_s, conv_s, bk_s, bv_s, _, _ = _trunk(
        x_sample, past, None, cache_mem_k, cache_mem_v, cache_mla_latent, cache_mla_kpe,
        state_conv, cache_band_k, cache_band_v, W, P, dims, bias_full)
    return (y_p, y_s, lat_p, kpe_p, conv_p, bk_p, bv_p, mk_p, mv_p, lat_s, kpe_s, conv_s, bk_s, bv_s)
```
